```python
import jax, jax.numpy as jnp
from jax import lax
import numpy as np

D_MODEL = 1024
BATCH = 4
SEQ = 8192
DEPTH = 4
DEC_BATCH = 16
DEC_SEQ = 64
PAST_LEN = 4096

CHUNK = 64
D_A = 512
D_B = 512
CONV_A = 31
CONV_B = 3
D_FF = 2816
CONV_F = 3
D_IN = 2 * D_A + 3 * D_B + 2 * D_MODEL
RMS_EPS = 1e-6
LN_EPS = 1e-5

kernel_name = "gated_parallel_conv_streaming_encoder"


def rms_norm(x, g):
    x32 = x.astype(jnp.float32)
    y = x32 * lax.rsqrt(jnp.mean(x32 * x32, axis=-1, keepdims=True) + RMS_EPS)
    return (y * g.astype(jnp.float32)).astype(x.dtype)


def layer_norm(x, g, b):
    x32 = x.astype(jnp.float32)
    mu = jnp.mean(x32, axis=-1, keepdims=True)
    xc = x32 - mu
    var = jnp.mean(xc * xc, axis=-1, keepdims=True)
    y = xc * lax.rsqrt(var + LN_EPS) * g.astype(jnp.float32) + b.astype(jnp.float32)
    return y.astype(x.dtype)


def causal_dwconv(h, prev, w):
    width, ch = w.shape
    hp = jnp.concatenate([prev.astype(h.dtype), h], axis=1)
    y = lax.conv_general_dilated(
        hp, w[:, None, :].astype(h.dtype), window_strides=(1,), padding="VALID",
        dimension_numbers=("NWC", "WIO", "NWC"), feature_group_count=ch)
    return y, hp[:, hp.shape[1] - (width - 1):, :]


def layer(x, st_a, st_b, st_f, norm1_g, w_in, b_gate, dw_a, b_dw_a, ln_a_g, ln_a_b, w_a_out,
          dw_b, w_b_out, w_o, norm2_g, w_up, dw_f, w_down):
    h = rms_norm(x, norm1_g)
    z = h @ w_in
    a_val, a_gate, b_g, c_g, b_val, gates = jnp.split(
        z, [D_A, 2 * D_A, 2 * D_A + D_B, 2 * D_A + 2 * D_B, 2 * D_A + 3 * D_B], axis=-1)
    u = a_val * jax.nn.sigmoid(a_gate)
    u, new_a = causal_dwconv(u, st_a, dw_a)
    u = jax.nn.silu(layer_norm(u + b_dw_a, ln_a_g, ln_a_b))
    y_a = u @ w_a_out
    v, new_b = causal_dwconv(c_g * b_val, st_b, dw_b)
    y_b = (b_g * v) @ w_b_out
    g = jax.nn.sigmoid(gates + b_gate)
    g_a, g_b = jnp.split(g, 2, axis=-1)
    x = x + (g_a * y_a + g_b * y_b) @ w_o
    h = rms_norm(x, norm2_g)
    f_gate, f_val = jnp.split(h @ w_up, 2, axis=-1)
    f_gate, new_f = causal_dwconv(f_gate, st_f, dw_f)
    x = x + (jax.nn.silu(f_gate) * f_val) @ w_down
    return x, new_a, new_b, new_f


def setup_inputs(seed: int = 0) -> dict:
    key = jax.random.key(seed)
    ks = jax.random.split(key, 24)
    f32 = jnp.float32
    nrm = lambda k, shape, s: jax.random.normal(k, shape, f32) * s
    return {
        "x_prompt": nrm(ks[0], (BATCH, SEQ, D_MODEL), 1.0),
        "x_sample": nrm(ks[1], (DEC_BATCH, DEC_SEQ, D_MODEL), 1.0),
        "cache_conv_a": nrm(ks[2], (DEPTH, DEC_BATCH, CONV_A - 1, D_A), 0.5),
        "cache_conv_b": nrm(ks[3], (DEPTH, DEC_BATCH, CONV_B - 1, D_B), 0.5),
        "cache_ffn_conv": nrm(ks[4], (DEPTH, DEC_BATCH, CONV_F - 1, D_FF), 0.5),
        "norm1_g": 1.0 + nrm(ks[5], (DEPTH, D_MODEL), 0.02),
        "w_in": nrm(ks[6], (DEPTH, D_MODEL, D_IN), D_MODEL ** -0.5),
        "b_gate": nrm(ks[7], (DEPTH, 2 * D_MODEL), 0.02),
        "dw_a": nrm(ks[8], (DEPTH, CONV_A, D_A), CONV_A ** -0.5),
        "b_dw_a": nrm(ks[9], (DEPTH, D_A), 0.02),
        "ln_a_g": 1.0 + nrm(ks[10], (DEPTH, D_A), 0.02),
        "ln_a_b": nrm(ks[11], (DEPTH, D_A), 0.02),
        "w_a_out": nrm(ks[12], (DEPTH, D_A, D_MODEL), D_A ** -0.5),
        "dw_b": nrm(ks[13], (DEPTH, CONV_B, D_B), CONV_B ** -0.5),
        "w_b_out": nrm(ks[14], (DEPTH, D_B, D_MODEL), D_B ** -0.5),
        "w_o": nrm(ks[15], (DEPTH, D_MODEL, D_MODEL), D_MODEL ** -0.5),
        "norm2_g": 1.0 + nrm(ks[16], (DEPTH, D_MODEL), 0.02),
        "w_up": nrm(ks[17], (DEPTH, D_MODEL, 2 * D_FF), D_MODEL ** -0.5),
        "dw_f": nrm(ks[18], (DEPTH, CONV_F, D_FF), CONV_F ** -0.5),
        "w_down": nrm(ks[19], (DEPTH, D_FF, D_MODEL), D_FF ** -0.5),
        "final_g": 1.0 + nrm(ks[20], (D_MODEL,), 0.02),
    }


def reference(x_prompt, x_sample, cache_conv_a, cache_conv_b, cache_ffn_conv, norm1_g, w_in, b_gate,
              dw_a, b_dw_a, ln_a_g, ln_a_b, w_a_out, dw_b, w_b_out, w_o, norm2_g, w_up, dw_f,
              w_down, final_g):
    n_p = x_prompt.shape[0]
    dt = x_prompt.dtype
    zero_a = jnp.zeros((n_p, CONV_A - 1, D_A), dt)
    zero_b = jnp.zeros((n_p, CONV_B - 1, D_B), dt)
    zero_f = jnp.zeros((n_p, CONV_F - 1, D_FF), dt)
    xp, xs = x_prompt, x_sample
    pa, pb, pf, sa, sb, sf = [], [], [], [], [], []
    for l in range(DEPTH):
        w_l = (norm1_g[l], w_in[l], b_gate[l], dw_a[l], b_dw_a[l], ln_a_g[l], ln_a_b[l], w_a_out[l],
               dw_b[l], w_b_out[l], w_o[l], norm2_g[l], w_up[l], dw_f[l], w_down[l])
        xp, na, nb, nf = layer(xp, zero_a, zero_b, zero_f, *w_l)
        pa.append(na); pb.append(nb); pf.append(nf)
        xs, na, nb, nf = layer(xs, cache_conv_a[l], cache_conv_b[l], cache_ffn_conv[l], *w_l)
        sa.append(na); sb.append(nb); sf.append(nf)
    y_prompt = rms_norm(xp, final_g)
    y_sample = rms_norm(xs, final_g)
    return (y_prompt, y_sample, jnp.stack(pa), jnp.stack(pb), jnp.stack(pf),
            jnp.stack(sa), jnp.stack(sb), jnp.stack(sf))
```

```python
import functools

import jax
import jax.numpy as jnp
from jax import lax
from jax.experimental import pallas as pl
from jax.experimental.pallas import tpu as pltpu

D_MODEL = 1024
D_A = 512
D_B = 512
CONV_A = 31
CONV_B = 3
D_FF = 2816
CONV_F = 3
D_IN = 2 * D_A + 3 * D_B + 2 * D_MODEL
RMS_EPS = 1e-6
LN_EPS = 1e-5

HIST_A = 32
HIST_3 = 8
CONV_ROWS = 32
FF_CHUNK = 256
VMEM_LIMIT_BYTES = 56 * 1024 * 1024

_F32 = jnp.float32
_BF16 = jnp.bfloat16


def _rms_norm(x, g):
    y = x * lax.rsqrt(jnp.mean(x * x, axis=-1, keepdims=True) + RMS_EPS)
    return y * g


def _dot(a, b):
    return jnp.dot(a, b, preferred_element_type=_F32)


def _mixer_kernel(x_ref, ca_ref, cb_ref, n1g_ref, w_in_ref, bgate_ref, dwa_ref, bdwa_ref, lng_ref,
                  lnb_ref, wa_ref, dwb_ref, wb_ref, wo_ref,
                  xo_ref, na_ref, nb_ref,
                  ua_ref, cv_ref, ya_ref, *, bt, t):
    ti = pl.program_id(1)
    m = bt * t

    @pl.when(ti == 0)
    def _():
        ua_ref[:, HIST_A - (CONV_A - 1):HIST_A, :] = ca_ref[...]
        cv_ref[:, HIST_3 - (CONV_B - 1):HIST_3, :] = cb_ref[...]

    @pl.when(ti > 0)
    def _():
        ua_ref[:, 0:HIST_A, :] = ua_ref[:, t:t + HIST_A, :]
        cv_ref[:, 0:HIST_3, :] = cv_ref[:, t:t + HIST_3, :]

    x = x_ref[...].reshape(m, D_MODEL)
    h = _rms_norm(x, n1g_ref[...]).astype(_BF16)

    za = _dot(h, w_in_ref[:, 0:2 * D_A])
    u = za[:, :D_A] * jax.nn.sigmoid(za[:, D_A:])
    ua_ref[:, HIST_A:HIST_A + t, :] = u.reshape(bt, t, D_A)
    na_ref[...] = ua_ref[:, t + HIST_A - (CONV_A - 1):t + HIST_A, :]

    tap0 = HIST_A - (CONV_A - 1)
    for b in range(bt):
        def conv_body(i, carry, b=b):
            r0 = pl.multiple_of(i * CONV_ROWS, CONV_ROWS)
            win = ua_ref.at[b, pl.ds(r0, CONV_ROWS + HIST_A), :]
            acc = jnp.zeros((CONV_ROWS, D_A), _F32)
            for k in range(CONV_A):
                acc = acc + win[tap0 + k:tap0 + k + CONV_ROWS, :] * dwa_ref[k:k + 1, :]
            acc = acc + bdwa_ref[...]
            mu = jnp.mean(acc, axis=-1, keepdims=True)
            xc = acc - mu
            var = jnp.mean(xc * xc, axis=-1, keepdims=True)
            y = xc * lax.rsqrt(var + LN_EPS) * lng_ref[...] + lnb_ref[...]
            y = y * jax.nn.sigmoid(y)
            ya_ref[pl.ds(b * t + r0, CONV_ROWS), :] = y.astype(_BF16)
            return carry
        lax.fori_loop(0, t // CONV_ROWS, conv_body, 0)
    y_a = _dot(ya_ref[...], wa_ref[...])

    zb = _dot(h, w_in_ref[:, 2 * D_A:2 * D_A + 3 * D_B])
    cv_ref[:, HIST_3:HIST_3 + t, :] = (zb[:, D_B:2 * D_B] * zb[:, 2 * D_B:]).reshape(bt, t, D_B)
    nb_ref[...] = cv_ref[:, t + HIST_3 - (CONV_B - 1):t + HIST_3, :]
    tap0b = HIST_3 - (CONV_B - 1)
    v = cv_ref[:, tap0b:tap0b + t, :] * dwb_ref[0:1, :][None]
    for k in range(1, CONV_B):
        v = v + cv_ref[:, tap0b + k:tap0b + k + t, :] * dwb_ref[k:k + 1, :][None]
    y_b = _dot((zb[:, :D_B] * v.reshape(m, D_B)).astype(_BF16), wb_ref[...])

    g = jax.nn.sigmoid(_dot(h, w_in_ref[:, 2 * D_A + 3 * D_B:]) + bgate_ref[...])
    mix = (g[:, :D_MODEL] * y_a + g[:, D_MODEL:] * y_b).astype(_BF16)
    xo_ref[...] = (x + _dot(mix, wo_ref[...])).reshape(bt, t, D_MODEL)


def _ffn_kernel(x_ref, cf_ref, n2g_ref, wup_ref, dwf_ref, wdn_ref, fg_ref,
                xo_ref, nf_ref,
                gw_ref, *, bt, t, final):
    ti = pl.program_id(1)
    m = bt * t

    @pl.when(ti == 0)
    def _():
        gw_ref[:, HIST_3 - (CONV_F - 1):HIST_3, :] = cf_ref[...]

    @pl.when(ti > 0)
    def _():
        gw_ref[:, 0:HIST_3, :] = gw_ref[:, t:t + HIST_3, :]

    x = x_ref[...].reshape(m, D_MODEL)
    h = _rms_norm(x, n2g_ref[...]).astype(_BF16)
    tap0 = HIST_3 - (CONV_F - 1)
    acc = x
    for c0 in range(0, D_FF, FF_CHUNK):
        c1 = c0 + FF_CHUNK
        gate = _dot(h, wup_ref[:, c0:c1])
        val = _dot(h, wup_ref[:, D_FF + c0:D_FF + c1])
        gw_ref[:, HIST_3:HIST_3 + t, c0:c1] = gate.reshape(bt, t, FF_CHUNK)
        gc = gw_ref[:, tap0:tap0 + t, c0:c1] * dwf_ref[0:1, c0:c1][None]
        for k in range(1, CONV_F):
            gc = gc + gw_ref[:, tap0 + k:tap0 + k + t, c0:c1] * dwf_ref[k:k + 1, c0:c1][None]
        gc = gc.reshape(m, FF_CHUNK)
        hh = (gc * jax.nn.sigmoid(gc) * val).astype(_BF16)
        acc = acc + _dot(hh, wdn_ref[c0:c1, :])
    nf_ref[...] = gw_ref[:, t + HIST_3 - (CONV_F - 1):t + HIST_3, :]
    if final:
        acc = _rms_norm(acc, fg_ref[...])
    xo_ref[...] = acc.reshape(bt, t, D_MODEL)


def _resident(shape, layer):
    zeros = (0,) * len(shape)
    return pl.BlockSpec((None,) + shape, lambda b, s: (layer,) + zeros, pipeline_mode=pl.Buffered(1))


def _tiles(n, s):
    t = min(s, 256)
    bt = max(1, min(n, 256 // t))
    assert n % bt == 0 and s % t == 0 and t % CONV_ROWS == 0 and t >= HIST_A
    return bt, t


def _mixer_layer(x, ca, cb, layer, p):
    n, s, _ = x.shape
    bt, t = _tiles(n, s)
    cache_spec = lambda rows, ch: pl.BlockSpec((bt, rows, ch), lambda b, i: (b, 0, 0))
    x_spec = pl.BlockSpec((bt, t, D_MODEL), lambda b, i: (b, i, 0))
    return pl.pallas_call(
        functools.partial(_mixer_kernel, bt=bt, t=t),
        grid=(n // bt, s // t),
        in_specs=[
            x_spec, cache_spec(CONV_A - 1, D_A), cache_spec(CONV_B - 1, D_B),
            _resident((1, D_MODEL), layer), _resident((D_MODEL, D_IN), layer),
            _resident((1, 2 * D_MODEL), layer), _resident((CONV_A, D_A), layer),
            _resident((1, D_A), layer), _resident((1, D_A), layer), _resident((1, D_A), layer),
            _resident((D_A, D_MODEL), layer), _resident((CONV_B, D_B), layer),
            _resident((D_B, D_MODEL), layer), _resident((D_MODEL, D_MODEL), layer),
        ],
        out_specs=[x_spec, cache_spec(CONV_A - 1, D_A), cache_spec(CONV_B - 1, D_B)],
        out_shape=[
            jax.ShapeDtypeStruct(x.shape, _F32),
            jax.ShapeDtypeStruct((n, CONV_A - 1, D_A), _F32),
            jax.ShapeDtypeStruct((n, CONV_B - 1, D_B), _F32),
        ],
        scratch_shapes=[
            pltpu.VMEM((bt, HIST_A + t, D_A), _F32),
            pltpu.VMEM((bt, HIST_3 + t, D_B), _F32),
            pltpu.VMEM((bt * t, D_A), _BF16),
        ],
        compiler_params=pltpu.CompilerParams(
            dimension_semantics=("arbitrary", "arbitrary"), vmem_limit_bytes=VMEM_LIMIT_BYTES),
        name="mixer",
    )(x, ca, cb, p["norm1_g"], p["w_in"], p["b_gate"], p["dw_a"], p["b_dw_a"], p["ln_a_g"],
      p["ln_a_b"], p["w_a_out"], p["dw_b"], p["w_b_out"], p["w_o"])


def _ffn_layer(x, cf, layer, p, final):
    n, s, _ = x.shape
    bt, t = _tiles(n, s)
    cache_spec = pl.BlockSpec((bt, CONV_F - 1, D_FF), lambda b, i: (b, 0, 0))
    x_spec = pl.BlockSpec((bt, t, D_MODEL), lambda b, i: (b, i, 0))
    return pl.pallas_call(
        functools.partial(_ffn_kernel, bt=bt, t=t, final=final),
        grid=(n // bt, s // t),
        in_specs=[
            x_spec, cache_spec,
            _resident((1, D_MODEL), layer), _resident((D_MODEL, 2 * D_FF), layer),
            _resident((CONV_F, D_FF), layer), _resident((D_FF, D_MODEL), layer),
            pl.BlockSpec((1, D_MODEL), lambda b, i: (0, 0)),
        ],
        out_specs=[x_spec, cache_spec],
        out_shape=[
            jax.ShapeDtypeStruct(x.shape, _F32),
            jax.ShapeDtypeStruct((n, CONV_F - 1, D_FF), _F32),
        ],
        scratch_shapes=[pltpu.VMEM((bt, HIST_3 + t, D_FF), _F32)],
        compiler_params=pltpu.CompilerParams(
            dimension_semantics=("arbitrary", "arbitrary"), vmem_limit_bytes=VMEM_LIMIT_BYTES),
        name="ffn",
    )(x, cf, p["norm2_g"], p["w_up"], p["dw_f"], p["w_down"], p["final_g"])


def kernel(x_prompt, x_sample, cache_conv_a, cache_conv_b, cache_ffn_conv, norm1_g, w_in, b_gate,
           dw_a, b_dw_a, ln_a_g, ln_a_b, w_a_out, dw_b, w_b_out, w_o, norm2_g, w_up, dw_f,
           w_down, final_g):
    depth = w_in.shape[0]
    n_p = x_prompt.shape[0]
    row = lambda a: a.reshape(a.shape[0], 1, a.shape[1])
    p = dict(
        norm1_g=row(norm1_g), w_in=w_in.astype(_BF16), b_gate=row(b_gate), dw_a=dw_a,
        b_dw_a=row(b_dw_a), ln_a_g=row(ln_a_g), ln_a_b=row(ln_a_b), w_a_out=w_a_out.astype(_BF16),
        dw_b=dw_b, w_b_out=w_b_out.astype(_BF16), w_o=w_o.astype(_BF16), norm2_g=row(norm2_g),
        w_up=w_up.astype(_BF16), dw_f=dw_f, w_down=w_down.astype(_BF16),
        final_g=final_g.reshape(1, D_MODEL),
    )
    zero_a = jnp.zeros((n_p, CONV_A - 1, D_A), _F32)
    zero_b = jnp.zeros((n_p, CONV_B - 1, D_B), _F32)
    zero_f = jnp.zeros((n_p, CONV_F - 1, D_FF), _F32)
    xp, xs = x_prompt, x_sample
    pa, pb, pf, sa, sb, sf = [], [], [], [], [], []
    for l in range(depth):
        final = l == depth - 1
        xp, na, nb = _mixer_layer(xp, zero_a, zero_b, l, p)
        xp, nf = _ffn_layer(xp, zero_f, l, p, final)
        pa.append(na); pb.append(nb); pf.append(nf)
        xs, na, nb = _mixer_layer(xs, cache_conv_a[l], cache_conv_b[l], l, p)
        xs, nf = _ffn_layer(xs, cache_ffn_conv[l], l, p, final)
        sa.append(na); sb.append(nb); sf.append(nf)
    return (xp, xs, jnp.stack(pa), jnp.stack(pb), jnp.stack(pf),
            jnp.stack(sa), jnp.stack(sb), jnp.stack(sf))
```

```python
import functools

import jax
import jax.numpy as jnp
from jax import lax
from jax.experimental import pallas as pl
from jax.experimental.pallas import tpu as pltpu

D_MODEL = 1024
D_A = 512
D_B = 512
CONV_A = 31
CONV_B = 3
D_FF = 2816
CONV_F = 3
D_IN = 2 * D_A + 3 * D_B + 2 * D_MODEL
RMS_EPS = 1e-6
LN_EPS = 1e-5

SUBLANES = 8
LANES = 128
HIST_A = 32
HIST_3 = CONV_B - 1
CONV_GROUPS = 8
FF_CHUNK = 256
PROMPT_ROWS = 256
VMEM_LIMIT_BYTES = 56 * 1024 * 1024

_F32 = jnp.float32
_BF16 = jnp.bfloat16


def _rms_norm(x, g):
    y = x * lax.rsqrt(jnp.mean(x * x, axis=-1, keepdims=True) + RMS_EPS)
    return y * g


def _dot(a, b):
    return jnp.dot(a, b, preferred_element_type=_F32)


def _history(prev_tail, tail, time_strided):
    if not time_strided:
        return prev_tail
    sub = lax.broadcasted_iota(jnp.int32, tail.shape, 1)
    return jnp.where(sub == 0, pltpu.roll(prev_tail, 1, axis=1), pltpu.roll(tail, 1, axis=1))


def _conv3(v, prev_ref, w_ref, c0, c1, time_strided):
    p = v.shape[0]
    tail = v[p - HIST_3:]
    hist = _history(prev_ref[:, :, c0:c1], tail, time_strided)
    prev_ref[:, :, c0:c1] = tail
    e = jnp.concatenate([hist, v], axis=0)
    out = e[0:p] * w_ref[0:1, c0:c1][None]
    for k in range(1, CONV_B):
        out = out + e[k:k + p] * w_ref[k:k + 1, c0:c1][None]
    return out


def _mixer_kernel(x_ref, ha_ref, hb_ref, n1g_ref, w_in_ref, bgate_ref, dwa_ref, bdwa_ref, lng_ref,
                  lnb_ref, wa_ref, dwb_ref, wb_ref, wo_ref,
                  xo_ref, ta_ref, tb_ref,
                  e_ref, c_ref, pb_ref, *, t, time_strided):
    ti = pl.program_id(1)
    p = t // SUBLANES

    n_lt = D_A // LANES

    @pl.when(ti == 0)
    def _():
        for j in range(n_lt):
            e_ref[j, p:p + HIST_A] = ha_ref[:, :, j * LANES:(j + 1) * LANES]
        pb_ref[...] = hb_ref[...]

    x = x_ref[...]
    h = _rms_norm(x, n1g_ref[...]).astype(_BF16)

    za = _dot(h, w_in_ref[:, 0:2 * D_A])
    u = (za[:, :D_A] * jax.nn.sigmoid(za[:, D_A:])).reshape(p, SUBLANES, D_A)
    tail = u[p - HIST_A:]
    ta_ref[...] = tail
    for j in range(n_lt):
        lanes = slice(j * LANES, (j + 1) * LANES)
        e_ref[j, 0:HIST_A] = _history(e_ref[j, p:p + HIST_A], tail[:, :, lanes], time_strided)
        e_ref[j, HIST_A:HIST_A + p] = u[:, :, lanes]

    tap0 = HIST_A - (CONV_A - 1)
    n_gb = p // CONV_GROUPS

    def conv_body(i, carry):
        j = i // n_gb
        g0 = pl.multiple_of((i % n_gb) * CONV_GROUPS, CONV_GROUPS)
        win = e_ref[j, pl.ds(g0, CONV_GROUPS + HIST_A)]
        acc = win[tap0:tap0 + CONV_GROUPS] * dwa_ref[j, 0:1, :][None]
        for k in range(1, CONV_A):
            acc = acc + win[tap0 + k:tap0 + k + CONV_GROUPS] * dwa_ref[j, k:k + 1, :][None]
        c_ref[j, pl.ds(g0, CONV_GROUPS)] = acc
        return carry

    lax.fori_loop(0, n_lt * n_gb, conv_body, 0)
    conv = jnp.concatenate([c_ref[j] for j in range(n_lt)], axis=-1).reshape(t, D_A) + bdwa_ref[...]
    mu = jnp.mean(conv, axis=-1, keepdims=True)
    xc = conv - mu
    var = jnp.mean(xc * xc, axis=-1, keepdims=True)
    ya = xc * lax.rsqrt(var + LN_EPS) * lng_ref[...] + lnb_ref[...]
    y_a = _dot((ya * jax.nn.sigmoid(ya)).astype(_BF16), wa_ref[...])

    zb = _dot(h, w_in_ref[:, 2 * D_A:2 * D_A + 3 * D_B])
    cv = (zb[:, D_B:2 * D_B] * zb[:, 2 * D_B:]).reshape(p, SUBLANES, D_B)
    tb_ref[...] = cv[p - HIST_3:]
    v = _conv3(cv, pb_ref, dwb_ref, 0, D_B, time_strided).reshape(t, D_B)
    y_b = _dot((zb[:, :D_B] * v).astype(_BF16), wb_ref[...])

    g = jax.nn.sigmoid(_dot(h, w_in_ref[:, 2 * D_A + 3 * D_B:]) + bgate_ref[...])
    mix = (g[:, :D_MODEL] * y_a + g[:, D_MODEL:] * y_b).astype(_BF16)
    xo_ref[...] = x + _dot(mix, wo_ref[...])


def _ffn_kernel(x_ref, hf_ref, n2g_ref, wup_ref, dwf_ref, wdn_ref, fg_ref,
                xo_ref, tf_ref,
                pf_ref, *, t, time_strided, final):
    ti = pl.program_id(1)
    p = t // SUBLANES

    @pl.when(ti == 0)
    def _():
        pf_ref[...] = hf_ref[...]

    x = x_ref[...]
    h = _rms_norm(x, n2g_ref[...]).astype(_BF16)
    acc = x
    up = lambda c0: _dot(h, wup_ref[:, 2 * c0:2 * (c0 + FF_CHUNK)])
    gv_next = up(0)
    for c0 in range(0, D_FF, FF_CHUNK):
        c1 = c0 + FF_CHUNK
        gv = gv_next
        if c1 < D_FF:
            gv_next = up(c1)
        gate = gv[:, :FF_CHUNK].reshape(p, SUBLANES, FF_CHUNK)
        tf_ref[:, :, c0:c1] = gate[p - HIST_3:]
        gc = _conv3(gate, pf_ref, dwf_ref, c0, c1, time_strided).reshape(t, FF_CHUNK)
        hh = (gc * jax.nn.sigmoid(gc) * gv[:, FF_CHUNK:]).astype(_BF16)
        acc = acc + _dot(hh, wdn_ref[c0:c1, :])
    if final:
        acc = _rms_norm(acc, fg_ref[...])
    xo_ref[...] = acc


def _resident(shape, layer):
    zeros = (0,) * len(shape)
    return pl.BlockSpec((None,) + shape, lambda b, s: (layer,) + zeros, pipeline_mode=pl.Buffered(1))


def _state_spec(groups, ch):
    return pl.BlockSpec((None, groups, SUBLANES, ch), lambda b, i: (b, 0, 0, 0))


def _mixer_layer(x, ha, hb, layer, p, t, time_strided):
    n, s, _ = x.shape
    x_spec = pl.BlockSpec((None, t, D_MODEL), lambda b, i: (b, i, 0))
    return pl.pallas_call(
        functools.partial(_mixer_kernel, t=t, time_strided=time_strided),
        grid=(n, s // t),
        in_specs=[
            x_spec, _state_spec(HIST_A, D_A), _state_spec(HIST_3, D_B),
            _resident((1, D_MODEL), layer), _resident((D_MODEL, D_IN), layer),
            _resident((1, 2 * D_MODEL), layer), _resident((D_A // LANES, CONV_A, LANES), layer),
            _resident((1, D_A), layer), _resident((1, D_A), layer), _resident((1, D_A), layer),
            _resident((D_A, D_MODEL), layer), _resident((CONV_B, D_B), layer),
            _resident((D_B, D_MODEL), layer), _resident((D_MODEL, D_MODEL), layer),
        ],
        out_specs=[x_spec, _state_spec(HIST_A, D_A), _state_spec(HIST_3, D_B)],
        out_shape=[
            jax.ShapeDtypeStruct(x.shape, _F32),
            jax.ShapeDtypeStruct((n, HIST_A, SUBLANES, D_A), _F32),
            jax.ShapeDtypeStruct((n, HIST_3, SUBLANES, D_B), _F32),
        ],
        scratch_shapes=[
            pltpu.VMEM((D_A // LANES, HIST_A + t // SUBLANES, SUBLANES, LANES), _F32),
            pltpu.VMEM((D_A // LANES, t // SUBLANES, SUBLANES, LANES), _F32),
            pltpu.VMEM((HIST_3, SUBLANES, D_B), _F32),
        ],
        compiler_params=pltpu.CompilerParams(
            dimension_semantics=("arbitrary", "arbitrary"), vmem_limit_bytes=VMEM_LIMIT_BYTES),
        name="mixer",
    )(x, ha, hb, p["norm1_g"], p["w_in"], p["b_gate"], p["dw_a"], p["b_dw_a"], p["ln_a_g"],
      p["ln_a_b"], p["w_a_out"], p["dw_b"], p["w_b_out"], p["w_o"])


def _ffn_layer(x, hf, layer, p, t, time_strided, final):
    n, s, _ = x.shape
    x_spec = pl.BlockSpec((None, t, D_MODEL), lambda b, i: (b, i, 0))
    return pl.pallas_call(
        functools.partial(_ffn_kernel, t=t, time_strided=time_strided, final=final),
        grid=(n, s // t),
        in_specs=[
            x_spec, _state_spec(HIST_3, D_FF),
            _resident((1, D_MODEL), layer), _resident((D_MODEL, 2 * D_FF), layer),
            _resident((CONV_F, D_FF), layer), _resident((D_FF, D_MODEL), layer),
            pl.BlockSpec((1, D_MODEL), lambda b, i: (0, 0)),
        ],
        out_specs=[x_spec, _state_spec(HIST_3, D_FF)],
        out_shape=[
            jax.ShapeDtypeStruct(x.shape, _F32),
            jax.ShapeDtypeStruct((n, HIST_3, SUBLANES, D_FF), _F32),
        ],
        scratch_shapes=[pltpu.VMEM((HIST_3, SUBLANES, D_FF), _F32)],
        compiler_params=pltpu.CompilerParams(
            dimension_semantics=("arbitrary", "arbitrary"), vmem_limit_bytes=VMEM_LIMIT_BYTES),
        name="ffn",
    )(x, hf, p["norm2_g"], p["w_up"], p["dw_f"], p["w_down"], p["final_g"])


class _Layout:
    def __init__(self, n, s, time_strided):
        self.n, self.s, self.time_strided = n, s, time_strided
        if time_strided:
            self.t = min(s, PROMPT_ROWS)
            self.blocks = n
            assert s % self.t == 0
        else:
            self.t = SUBLANES * s
            self.blocks = n // SUBLANES
            assert n % SUBLANES == 0
        self.p = self.t // SUBLANES
        assert self.p >= HIST_A and self.p % CONV_GROUPS == 0

    def to_rows(self, x):
        c = x.shape[-1]
        if self.time_strided:
            y = x.reshape(self.n, self.s // self.t, SUBLANES, self.p, c).transpose(0, 1, 3, 2, 4)
            return y.reshape(self.n, self.s, c)
        y = x.reshape(self.blocks, SUBLANES, self.s, c).transpose(0, 2, 1, 3)
        return y.reshape(self.blocks, self.t, c)

    def from_rows(self, y):
        c = y.shape[-1]
        if self.time_strided:
            x = y.reshape(self.n, self.s // self.t, self.p, SUBLANES, c).transpose(0, 1, 3, 2, 4)
            return x.reshape(self.n, self.s, c)
        x = y.reshape(self.blocks, self.s, SUBLANES, c).transpose(0, 2, 1, 3)
        return x.reshape(self.n, self.s, c)

    def state_in(self, cache, groups):
        n, w, c = cache.shape
        if self.time_strided:
            st = jnp.zeros((n, groups, SUBLANES, c), cache.dtype)
            return st.at[:, groups - w:, SUBLANES - 1, :].set(cache)
        st = cache.reshape(self.blocks, SUBLANES, w, c).transpose(0, 2, 1, 3)
        return jnp.pad(st, ((0, 0), (groups - w, 0), (0, 0), (0, 0)))

    def state_out(self, tail, w):
        groups, c = tail.shape[1], tail.shape[-1]
        if self.time_strided:
            return tail[:, groups - w:, SUBLANES - 1, :]
        return tail[:, groups - w:].transpose(0, 2, 1, 3).reshape(self.n, w, c)


def kernel(x_prompt, x_sample, cache_conv_a, cache_conv_b, cache_ffn_conv, norm1_g, w_in, b_gate,
           dw_a, b_dw_a, ln_a_g, ln_a_b, w_a_out, dw_b, w_b_out, w_o, norm2_g, w_up, dw_f,
           w_down, final_g):
    depth = w_in.shape[0]
    n_p = x_prompt.shape[0]
    row = lambda a: a.reshape(a.shape[0], 1, a.shape[1])
    n_chunks = D_FF // FF_CHUNK
    w_up_c = w_up.reshape(depth, D_MODEL, 2, n_chunks, FF_CHUNK).transpose(0, 1, 3, 2, 4)
    p = dict(
        norm1_g=row(norm1_g), w_in=w_in.astype(_BF16), b_gate=row(b_gate),
        dw_a=dw_a.reshape(depth, CONV_A, D_A // LANES, LANES).transpose(0, 2, 1, 3),
        b_dw_a=row(b_dw_a), ln_a_g=row(ln_a_g), ln_a_b=row(ln_a_b), w_a_out=w_a_out.astype(_BF16),
        dw_b=dw_b, w_b_out=w_b_out.astype(_BF16), w_o=w_o.astype(_BF16), norm2_g=row(norm2_g),
        w_up=w_up_c.reshape(depth, D_MODEL, 2 * D_FF).astype(_BF16), dw_f=dw_f,
        w_down=w_down.astype(_BF16), final_g=final_g.reshape(1, D_MODEL),
    )
    groups = [
        (_Layout(n_p, x_prompt.shape[1], True), x_prompt,
         lambda l: (jnp.zeros((n_p, CONV_A - 1, D_A), _F32), jnp.zeros((n_p, CONV_B - 1, D_B), _F32),
                    jnp.zeros((n_p, CONV_F - 1, D_FF), _F32))),
        (_Layout(x_sample.shape[0], x_sample.shape[1], False), x_sample,
         lambda l: (cache_conv_a[l], cache_conv_b[l], cache_ffn_conv[l])),
    ]
    ys, new_a, new_b, new_f = [], [], [], []
    for lay, x, caches in groups:
        xr = lay.to_rows(x)
        sa, sb, sf = [], [], []
        for l in range(depth):
            ca, cb, cf = caches(l)
            xr, ta, tb = _mixer_layer(xr, lay.state_in(ca, HIST_A), lay.state_in(cb, HIST_3), l, p,
                                      lay.t, lay.time_strided)
            xr, tf = _ffn_layer(xr, lay.state_in(cf, HIST_3), l, p, lay.t, lay.time_strided,
                                l == depth - 1)
            sa.append(lay.state_out(ta, CONV_A - 1))
            sb.append(lay.state_out(tb, CONV_B - 1))
            sf.append(lay.state_out(tf, CONV_F - 1))
        ys.append(lay.from_rows(xr))
        new_a.append(jnp.stack(sa)); new_b.append(jnp.stack(sb)); new_f.append(jnp.stack(sf))
    return (ys[0], ys[1], new_a[0], new_b[0], new_f[0], new_a[1], new_b[1], new_f[1])
```

```python
import functools

import jax
import jax.numpy as jnp
from jax import lax
from jax.experimental import pallas as pl
from jax.experimental.pallas import tpu as pltpu

D_MODEL = 1024
D_A = 512
D_B = 512
CONV_A = 31
CONV_B = 3
D_FF = 2816
CONV_F = 3
RMS_EPS = 1e-6
LN_EPS = 1e-5

SUBLANES = 8
LANES = 128
HIST_A = 32
HIST_3 = CONV_B - 1
CONV_GROUPS = 8
FF_CHUNK = 256
GATE_CHUNKS = 8
GATES_PER_TRIP = 2
BLOCK_ROWS = 256
VMEM_LIMIT_BYTES = 56 * 1024 * 1024

_F32 = jnp.float32
_BF16 = jnp.bfloat16


def _rms_norm(x, g):
    y = x * lax.rsqrt(jnp.mean(x * x, axis=-1, keepdims=True) + RMS_EPS)
    return y * g


def _dot(a, b):
    return jnp.dot(a, b, preferred_element_type=_F32)


def _history(prev_tail, tail, time_strided):
    if not time_strided:
        return prev_tail
    sub = lax.broadcasted_iota(jnp.int32, tail.shape, 1)
    return jnp.where(sub == 0, pltpu.roll(prev_tail, 1, axis=1), pltpu.roll(tail, 1, axis=1))


def _conv3(v, prev_ref, w_ref, c0, c1, time_strided):
    p = v.shape[0]
    tail = v[p - HIST_3:]
    hist = _history(prev_ref[:, :, c0:c1], tail, time_strided)
    prev_ref[:, :, c0:c1] = tail
    e = jnp.concatenate([hist, v], axis=0)
    out = e[0:p] * w_ref[0:1, c0:c1][None]
    for k in range(1, CONV_B):
        out = out + e[k:k + p] * w_ref[k:k + 1, c0:c1][None]
    return out


def _layer_kernel(x_ref, ha_ref, hb_ref, hf_ref,
                  n1g_ref, wmain_ref, wgate_ref, bgate_ref, dwa_ref, bdwa_ref, lng_ref, lnb_ref, wa_ref,
                  dwb_ref, wb_ref, wo_ref, n2g_ref, wup_ref, dwf_ref, wdn_ref, fg_ref,
                  xo_ref, ta_ref, tb_ref, tf_ref,
                  h_ref, e_ref, c_ref, g_ref, pb_ref, pf_ref, *, t, time_strided, permute_in, permute_out,
                  final):
    ti = pl.program_id(1)
    p = t // SUBLANES
    n_lt = D_A // LANES

    @pl.when(ti == 0)
    def _():
        for j in range(n_lt):
            e_ref[j, p:p + HIST_A] = ha_ref[:, :, j * LANES:(j + 1) * LANES]
        pb_ref[...] = hb_ref[...]
        pf_ref[...] = hf_ref[...]

    x = x_ref[...]
    if permute_in:
        x = pltpu.einshape("sgd->gsd", x.reshape(SUBLANES, p, D_MODEL)).reshape(t, D_MODEL)
    h = _rms_norm(x, n1g_ref[...]).astype(_BF16)
    h_ref[...] = h
    za = _dot(h, wmain_ref[:, 0:2 * D_A])
    zb = _dot(h, wmain_ref[:, 2 * D_A:])

    u = (za[:, :D_A] * jax.nn.sigmoid(za[:, D_A:])).reshape(p, SUBLANES, D_A)
    tail = u[p - HIST_A:]
    ta_ref[...] = tail
    for j in range(n_lt):
        lanes = slice(j * LANES, (j + 1) * LANES)
        e_ref[j, 0:HIST_A] = _history(e_ref[j, p:p + HIST_A], tail[:, :, lanes], time_strided)
        e_ref[j, HIST_A:HIST_A + p] = u[:, :, lanes]

    cv = (zb[:, D_B:2 * D_B] * zb[:, 2 * D_B:]).reshape(p, SUBLANES, D_B)
    tb_ref[...] = cv[p - HIST_3:]
    v = _conv3(cv, pb_ref, dwb_ref, 0, D_B, time_strided).reshape(t, D_B)
    yb_in = (zb[:, :D_B] * v).astype(_BF16)

    tap0 = HIST_A - (CONV_A - 1)
    n_gb = p // CONV_GROUPS
    blocks_per_trip = n_lt * n_gb * GATES_PER_TRIP // GATE_CHUNKS

    def conv_body(i, carry):
        for q in range(GATES_PER_TRIP):
            c = i * GATES_PER_TRIP + q
            g_ref[c] = _dot(h_ref[...], wgate_ref[c]) + bgate_ref[c]
        for q in range(blocks_per_trip):
            blk = i * blocks_per_trip + q
            j = blk // n_gb
            g0 = pl.multiple_of((blk % n_gb) * CONV_GROUPS, CONV_GROUPS)
            win = e_ref[j, pl.ds(g0, CONV_GROUPS + HIST_A)]
            acc = win[tap0:tap0 + CONV_GROUPS] * dwa_ref[j, 0:1, :][None]
            for k in range(1, CONV_A):
                acc = acc + win[tap0 + k:tap0 + k + CONV_GROUPS] * dwa_ref[j, k:k + 1, :][None]
            c_ref[j, pl.ds(g0, CONV_GROUPS)] = acc
        return carry

    lax.fori_loop(0, GATE_CHUNKS // GATES_PER_TRIP, conv_body, 0)

    y_b = _dot(yb_in, wb_ref[...])
    conv = jnp.concatenate([c_ref[j] for j in range(n_lt)], axis=-1).reshape(t, D_A) + bdwa_ref[...]
    mu = jnp.mean(conv, axis=-1, keepdims=True)
    xc = conv - mu
    var = jnp.mean(xc * xc, axis=-1, keepdims=True)
    ya = xc * lax.rsqrt(var + LN_EPS) * lng_ref[...] + lnb_ref[...]
    y_a = _dot((ya * jax.nn.sigmoid(ya)).astype(_BF16), wa_ref[...])

    half = GATE_CHUNKS // 2
    g_a = jax.nn.sigmoid(jnp.concatenate([g_ref[c] for c in range(half)], axis=-1))
    g_b = jax.nn.sigmoid(jnp.concatenate([g_ref[c] for c in range(half, GATE_CHUNKS)], axis=-1))
    mix = (g_a * y_a + g_b * y_b).astype(_BF16)
    x1 = x + _dot(mix, wo_ref[...])

    h2 = _rms_norm(x1, n2g_ref[...]).astype(_BF16)
    acc = x1
    up = lambda c0: _dot(h2, wup_ref[:, 2 * c0:2 * (c0 + FF_CHUNK)])
    gv_next = up(0)
    for c0 in range(0, D_FF, FF_CHUNK):
        c1 = c0 + FF_CHUNK
        gv = gv_next
        if c1 < D_FF:
            gv_next = up(c1)
        gate = gv[:, :FF_CHUNK].reshape(p, SUBLANES, FF_CHUNK)
        tf_ref[:, :, c0:c1] = gate[p - HIST_3:]
        gc = _conv3(gate, pf_ref, dwf_ref, c0, c1, time_strided).reshape(t, FF_CHUNK)
        hh = (gc * jax.nn.sigmoid(gc) * gv[:, FF_CHUNK:]).astype(_BF16)
        acc = acc + _dot(hh, wdn_ref[c0:c1, :])
    if final:
        acc = _rms_norm(acc, fg_ref[...])
    if permute_out:
        acc = pltpu.einshape("gsd->sgd", acc.reshape(p, SUBLANES, D_MODEL)).reshape(t, D_MODEL)
    xo_ref[...] = acc


def _resident(shape, layer):
    zeros = (0,) * len(shape)
    return pl.BlockSpec((None,) + shape, lambda b, s: (layer,) + zeros, pipeline_mode=pl.Buffered(1))


def _state_spec(groups, ch):
    return pl.BlockSpec((None, groups, SUBLANES, ch), lambda b, i: (b, 0, 0, 0))


def _layer(x, ha, hb, hf, layer, p, t, time_strided, permute_in, permute_out, final):
    n, s, _ = x.shape
    x_spec = pl.BlockSpec((None, t, D_MODEL), lambda b, i: (b, i, 0))
    groups = t // SUBLANES
    n_lt = D_A // LANES
    gate_w = 2 * D_MODEL // GATE_CHUNKS
    return pl.pallas_call(
        functools.partial(_layer_kernel, t=t, time_strided=time_strided, permute_in=permute_in,
                          permute_out=permute_out, final=final),
        grid=(n, s // t),
        in_specs=[
            x_spec, _state_spec(HIST_A, D_A), _state_spec(HIST_3, D_B), _state_spec(HIST_3, D_FF),
            _resident((1, D_MODEL), layer), _resident((D_MODEL, 2 * D_A + 3 * D_B), layer),
            _resident((GATE_CHUNKS, D_MODEL, gate_w), layer), _resident((GATE_CHUNKS, 1, gate_w), layer),
            _resident((n_lt, CONV_A, LANES), layer),
            _resident((1, D_A), layer), _resident((1, D_A), layer), _resident((1, D_A), layer),
            _resident((D_A, D_MODEL), layer), _resident((CONV_B, D_B), layer),
            _resident((D_B, D_MODEL), layer), _resident((D_MODEL, D_MODEL), layer),
            _resident((1, D_MODEL), layer), _resident((D_MODEL, 2 * D_FF), layer),
            _resident((CONV_F, D_FF), layer), _resident((D_FF, D_MODEL), layer),
            pl.BlockSpec((1, D_MODEL), lambda b, i: (0, 0)),
        ],
        out_specs=[x_spec, _state_spec(HIST_A, D_A), _state_spec(HIST_3, D_B), _state_spec(HIST_3, D_FF)],
        out_shape=[
            jax.ShapeDtypeStruct(x.shape, _F32),
            jax.ShapeDtypeStruct((n, HIST_A, SUBLANES, D_A), _F32),
            jax.ShapeDtypeStruct((n, HIST_3, SUBLANES, D_B), _F32),
            jax.ShapeDtypeStruct((n, HIST_3, SUBLANES, D_FF), _F32),
        ],
        scratch_shapes=[
            pltpu.VMEM((t, D_MODEL), _BF16),
            pltpu.VMEM((n_lt, HIST_A + groups, SUBLANES, LANES), _F32),
            pltpu.VMEM((n_lt, groups, SUBLANES, LANES), _F32),
            pltpu.VMEM((GATE_CHUNKS, t, gate_w), _F32),
            pltpu.VMEM((HIST_3, SUBLANES, D_B), _F32),
            pltpu.VMEM((HIST_3, SUBLANES, D_FF), _F32),
        ],
        compiler_params=pltpu.CompilerParams(
            dimension_semantics=("arbitrary", "arbitrary"), vmem_limit_bytes=VMEM_LIMIT_BYTES),
        name="layer",
    )(x, ha, hb, hf, p["norm1_g"], p["w_main"], p["w_gate"], p["b_gate"], p["dw_a"], p["b_dw_a"],
      p["ln_a_g"], p["ln_a_b"], p["w_a_out"], p["dw_b"], p["w_b_out"], p["w_o"], p["norm2_g"], p["w_up"],
      p["dw_f"], p["w_down"], p["final_g"])


class _Layout:
    def __init__(self, n, s, time_strided):
        self.n, self.s, self.time_strided = n, s, time_strided
        if time_strided:
            self.t = min(s, BLOCK_ROWS)
            self.blocks = n
            assert s % self.t == 0
        else:
            self.t = min(SUBLANES * s, BLOCK_ROWS)
            self.blocks = n // SUBLANES
            assert n % SUBLANES == 0 and (SUBLANES * s) % self.t == 0
        self.p = self.t // SUBLANES
        assert self.p >= HIST_A and self.p % CONV_GROUPS == 0

    def to_rows(self, x):
        c = x.shape[-1]
        if self.time_strided:
            return x
        y = x.reshape(self.blocks, SUBLANES, self.s, c).transpose(0, 2, 1, 3)
        return y.reshape(self.blocks, SUBLANES * self.s, c)

    def from_rows(self, y):
        c = y.shape[-1]
        if self.time_strided:
            return y
        x = y.reshape(self.blocks, self.s, SUBLANES, c).transpose(0, 2, 1, 3)
        return x.reshape(self.n, self.s, c)

    def state_in(self, cache, groups):
        n, w, c = cache.shape
        if self.time_strided:
            st = jnp.zeros((n, groups, SUBLANES, c), cache.dtype)
            return st.at[:, groups - w:, SUBLANES - 1, :].set(cache)
        st = cache.reshape(self.blocks, SUBLANES, w, c).transpose(0, 2, 1, 3)
        return jnp.pad(st, ((0, 0), (groups - w, 0), (0, 0), (0, 0)))

    def state_out(self, tail, w):
        groups, c = tail.shape[1], tail.shape[-1]
        if self.time_strided:
            return tail[:, groups - w:, SUBLANES - 1, :]
        return tail[:, groups - w:].transpose(0, 2, 1, 3).reshape(self.n, w, c)


def kernel(x_prompt, x_sample, cache_conv_a, cache_conv_b, cache_ffn_conv, norm1_g, w_in, b_gate,
           dw_a, b_dw_a, ln_a_g, ln_a_b, w_a_out, dw_b, w_b_out, w_o, norm2_g, w_up, dw_f,
           w_down, final_g):
    depth = w_in.shape[0]
    n_p = x_prompt.shape[0]
    row = lambda a: a.reshape(a.shape[0], 1, a.shape[1])
    n_chunks = D_FF // FF_CHUNK
    n_main = 2 * D_A + 3 * D_B
    gate_w = 2 * D_MODEL // GATE_CHUNKS
    w_up_c = w_up.reshape(depth, D_MODEL, 2, n_chunks, FF_CHUNK).transpose(0, 1, 3, 2, 4)
    p = dict(
        norm1_g=row(norm1_g), w_main=w_in[:, :, :n_main].astype(_BF16),
        w_gate=w_in[:, :, n_main:].reshape(depth, D_MODEL, GATE_CHUNKS, gate_w).transpose(0, 2, 1, 3).astype(_BF16),
        b_gate=b_gate.reshape(depth, GATE_CHUNKS, 1, gate_w),
        dw_a=dw_a.reshape(depth, CONV_A, D_A // LANES, LANES).transpose(0, 2, 1, 3),
        b_dw_a=row(b_dw_a), ln_a_g=row(ln_a_g), ln_a_b=row(ln_a_b), w_a_out=w_a_out.astype(_BF16),
        dw_b=dw_b, w_b_out=w_b_out.astype(_BF16), w_o=w_o.astype(_BF16), norm2_g=row(norm2_g),
        w_up=w_up_c.reshape(depth, D_MODEL, 2 * D_FF).astype(_BF16), dw_f=dw_f,
        w_down=w_down.astype(_BF16), final_g=final_g.reshape(1, D_MODEL),
    )
    groups = [
        (_Layout(n_p, x_prompt.shape[1], True), x_prompt,
         lambda l: (jnp.zeros((n_p, CONV_A - 1, D_A), _F32), jnp.zeros((n_p, CONV_B - 1, D_B), _F32),
                    jnp.zeros((n_p, CONV_F - 1, D_FF), _F32))),
        (_Layout(x_sample.shape[0], x_sample.shape[1], False), x_sample,
         lambda l: (cache_conv_a[l], cache_conv_b[l], cache_ffn_conv[l])),
    ]
    ys, new_a, new_b, new_f = [], [], [], []
    for lay, x, caches in groups:
        xr = lay.to_rows(x)
        sa, sb, sf = [], [], []
        for l in range(depth):
            ca, cb, cf = caches(l)
            xr, ta, tb, tf = _layer(xr, lay.state_in(ca, HIST_A), lay.state_in(cb, HIST_3),
                                    lay.state_in(cf, HIST_3), l, p, lay.t, lay.time_strided,
                                    lay.time_strided and l == 0, lay.time_strided and l == depth - 1,
                                    l == depth - 1)
            sa.append(lay.state_out(ta, CONV_A - 1))
            sb.append(lay.state_out(tb, CONV_B - 1))
            sf.append(lay.state_out(tf, CONV_F - 1))
        ys.append(lay.from_rows(xr))
        new_a.append(jnp.stack(sa)); new_b.append(jnp.stack(sb)); new_f.append(jnp.stack(sf))
    return (ys[0], ys[1], new_a[0], new_b[0], new_f[0], new_a[1], new_b[1], new_f[1])
```

```python
import functools

import jax
import jax.numpy as jnp
from jax import lax
from jax.experimental import pallas as pl
from jax.experimental.pallas import tpu as pltpu

D_MODEL = 1024
D_A = 512
D_B = 512
CONV_A = 31
CONV_B = 3
D_FF = 2816
CONV_F = 3
RMS_EPS = 1e-6
LN_EPS = 1e-5

SUBLANES = 8
LANES = 128
HIST_A = 32
HIST_3 = CONV_B - 1
CONV_GROUPS = 8
FF_CHUNK = 256
GATE_CHUNKS = 8
GATES_PER_TRIP = 4
GATE_W = 2 * D_MODEL // GATE_CHUNKS
GATE_COL0 = 2 * D_A + 3 * D_B
D_IN = GATE_COL0 + 2 * D_MODEL
BLOCK_ROWS = 256
VMEM_LIMIT_BYTES = 56 * 1024 * 1024

_F32 = jnp.float32
_BF16 = jnp.bfloat16


def _rms_norm(x, g):
    y = x * lax.rsqrt(jnp.mean(x * x, axis=-1, keepdims=True) + RMS_EPS)
    return y * g


def _dot(a, b):
    return jnp.dot(a, b, preferred_element_type=_F32)


def _history(prev_tail, tail, time_strided):
    if not time_strided:
        return prev_tail
    sub = lax.broadcasted_iota(jnp.int32, tail.shape, 1)
    return jnp.where(sub == 0, pltpu.roll(prev_tail, 1, axis=1), pltpu.roll(tail, 1, axis=1))


def _conv3(v, prev_ref, w_ref, c0, c1, time_strided):
    p = v.shape[0]
    tail = v[p - HIST_3:]
    hist = _history(prev_ref[:, :, c0:c1], tail, time_strided)
    prev_ref[:, :, c0:c1] = tail
    e = jnp.concatenate([hist, v], axis=0)
    out = e[0:p] * w_ref[0:1, c0:c1][None]
    for k in range(1, CONV_B):
        out = out + e[k:k + p] * w_ref[k:k + 1, c0:c1][None]
    return out


def _layer_kernel(x_ref, ha_ref, hb_ref, hf_ref,
                  n1g_ref, win_ref, bgate_ref, dwa_ref, bdwa_ref, lng_ref, lnb_ref, wa_ref,
                  dwb_ref, wb_ref, wo_ref, n2g_ref, wup_ref, dwf_ref, wdn_ref, fg_ref,
                  xo_ref, ta_ref, tb_ref, tf_ref,
                  h_ref, e_ref, c_ref, g_ref, pb_ref, pf_ref, *, t, time_strided, permute_in, permute_out,
                  final):
    ti = pl.program_id(1)
    p = t // SUBLANES
    n_lt = D_A // LANES

    @pl.when(ti == 0)
    def _():
        for j in range(n_lt):
            e_ref[j, p:p + HIST_A] = ha_ref[:, :, j * LANES:(j + 1) * LANES]
        pb_ref[...] = hb_ref[...]
        pf_ref[...] = hf_ref[...]

    x = x_ref[...]
    if permute_in:
        x = pltpu.einshape("sgd->gsd", x.reshape(SUBLANES, p, D_MODEL)).reshape(t, D_MODEL)
    h = _rms_norm(x, n1g_ref[...]).astype(_BF16)
    h_ref[...] = h
    za = _dot(h, win_ref[:, 0:2 * D_A])
    zb = _dot(h, win_ref[:, 2 * D_A:GATE_COL0])

    u = (za[:, :D_A] * jax.nn.sigmoid(za[:, D_A:])).reshape(p, SUBLANES, D_A)
    tail = u[p - HIST_A:]
    ta_ref[...] = tail
    for j in range(n_lt):
        lanes = slice(j * LANES, (j + 1) * LANES)
        e_ref[j, 0:HIST_A] = _history(e_ref[j, p:p + HIST_A], tail[:, :, lanes], time_strided)
        e_ref[j, HIST_A:HIST_A + p] = u[:, :, lanes]

    cv = (zb[:, D_B:2 * D_B] * zb[:, 2 * D_B:]).reshape(p, SUBLANES, D_B)
    tb_ref[...] = cv[p - HIST_3:]
    v = _conv3(cv, pb_ref, dwb_ref, 0, D_B, time_strided).reshape(t, D_B)
    yb_in = (zb[:, :D_B] * v).astype(_BF16)

    tap0 = HIST_A - (CONV_A - 1)
    n_gb = p // CONV_GROUPS
    blocks_per_trip = n_lt * n_gb * GATES_PER_TRIP // GATE_CHUNKS

    def gate_chunk(c, after):
        lhs = h_ref[...]
        if after is not None:
            zero = (pltpu.bitcast(after[0], jnp.uint32) >> 16) >> 16
            top = pltpu.bitcast(pltpu.bitcast(lhs[0:16, 0:LANES], jnp.uint32) | zero, _BF16)
            lhs = jnp.concatenate(
                [jnp.concatenate([top, lhs[0:16, LANES:]], axis=1), lhs[16:]], axis=0)
        col = pl.multiple_of(c * GATE_W, GATE_W)
        g_ref[c] = _dot(lhs, win_ref[:, pl.ds(GATE_COL0 + col, GATE_W)]) + bgate_ref[:, pl.ds(col, GATE_W)]

    def conv_block(blk):
        j = blk // n_gb
        g0 = pl.multiple_of((blk % n_gb) * CONV_GROUPS, CONV_GROUPS)
        win = e_ref[j, pl.ds(g0, CONV_GROUPS + HIST_A)]
        acc = win[tap0:tap0 + CONV_GROUPS] * dwa_ref[j, 0:1, :][None]
        for k in range(1, CONV_A):
            acc = acc + win[tap0 + k:tap0 + k + CONV_GROUPS] * dwa_ref[j, k:k + 1, :][None]
        c_ref[j, pl.ds(g0, CONV_GROUPS)] = acc
        return acc

    blocks_per_gate = blocks_per_trip // GATES_PER_TRIP

    def conv_body(i, carry):
        after = None
        for q in range(GATES_PER_TRIP):
            gate_chunk(i * GATES_PER_TRIP + q, after)
            for r in range(blocks_per_gate):
                after = conv_block(i * blocks_per_trip + q * blocks_per_gate + r)
        return carry

    lax.fori_loop(0, GATE_CHUNKS // GATES_PER_TRIP, conv_body, 0)

    y_b = _dot(yb_in, wb_ref[...])
    conv = jnp.concatenate([c_ref[j] for j in range(n_lt)], axis=-1).reshape(t, D_A) + bdwa_ref[...]
    mu = jnp.mean(conv, axis=-1, keepdims=True)
    xc = conv - mu
    var = jnp.mean(xc * xc, axis=-1, keepdims=True)
    ya = xc * lax.rsqrt(var + LN_EPS) * lng_ref[...] + lnb_ref[...]
    y_a = _dot((ya * jax.nn.sigmoid(ya)).astype(_BF16), wa_ref[...])

    half = GATE_CHUNKS // 2
    g_a = jax.nn.sigmoid(jnp.concatenate([g_ref[c] for c in range(half)], axis=-1))
    g_b = jax.nn.sigmoid(jnp.concatenate([g_ref[c] for c in range(half, GATE_CHUNKS)], axis=-1))
    mix = (g_a * y_a + g_b * y_b).astype(_BF16)
    x1 = x + _dot(mix, wo_ref[...])

    h2 = _rms_norm(x1, n2g_ref[...]).astype(_BF16)
    acc = x1
    def up(c0):
        w = jnp.concatenate([wup_ref[:, c0:c0 + FF_CHUNK], wup_ref[:, D_FF + c0:D_FF + c0 + FF_CHUNK]], axis=1)
        return _dot(h2, w)

    gv_next = up(0)
    for c0 in range(0, D_FF, FF_CHUNK):
        c1 = c0 + FF_CHUNK
        gv = gv_next
        if c1 < D_FF:
            gv_next = up(c1)
        gate = gv[:, :FF_CHUNK].reshape(p, SUBLANES, FF_CHUNK)
        tf_ref[:, :, c0:c1] = gate[p - HIST_3:]
        gc = _conv3(gate, pf_ref, dwf_ref, c0, c1, time_strided).reshape(t, FF_CHUNK)
        hh = (gc * jax.nn.sigmoid(gc) * gv[:, FF_CHUNK:]).astype(_BF16)
        acc = acc + _dot(hh, wdn_ref[c0:c1, :])
    if final:
        acc = _rms_norm(acc, fg_ref[...])
    if permute_out:
        acc = pltpu.einshape("gsd->sgd", acc.reshape(p, SUBLANES, D_MODEL)).reshape(t, D_MODEL)
    xo_ref[...] = acc


def _resident(shape, layer):
    zeros = (0,) * len(shape)
    return pl.BlockSpec((None,) + shape, lambda b, s: (layer,) + zeros, pipeline_mode=pl.Buffered(1))


def _state_spec(groups, ch):
    return pl.BlockSpec((None, groups, SUBLANES, ch), lambda b, i: (b, 0, 0, 0))


def _layer(x, ha, hb, hf, layer, p, t, time_strided, permute_in, permute_out, final):
    n, s, _ = x.shape
    x_spec = pl.BlockSpec((None, t, D_MODEL), lambda b, i: (b, i, 0))
    groups = t // SUBLANES
    n_lt = D_A // LANES
    return pl.pallas_call(
        functools.partial(_layer_kernel, t=t, time_strided=time_strided, permute_in=permute_in,
                          permute_out=permute_out, final=final),
        grid=(n, s // t),
        in_specs=[
            x_spec, _state_spec(HIST_A, D_A), _state_spec(HIST_3, D_B), _state_spec(HIST_3, D_FF),
            _resident((1, D_MODEL), layer), _resident((D_MODEL, D_IN), layer),
            _resident((1, 2 * D_MODEL), layer),
            _resident((n_lt, CONV_A, LANES), layer),
            _resident((1, D_A), layer), _resident((1, D_A), layer), _resident((1, D_A), layer),
            _resident((D_A, D_MODEL), layer), _resident((CONV_B, D_B), layer),
            _resident((D_B, D_MODEL), layer), _resident((D_MODEL, D_MODEL), layer),
            _resident((1, D_MODEL), layer), _resident((D_MODEL, 2 * D_FF), layer),
            _resident((CONV_F, D_FF), layer), _resident((D_FF, D_MODEL), layer),
            pl.BlockSpec((1, D_MODEL), lambda b, i: (0, 0)),
        ],
        out_specs=[x_spec, _state_spec(HIST_A, D_A), _state_spec(HIST_3, D_B), _state_spec(HIST_3, D_FF)],
        out_shape=[
            jax.ShapeDtypeStruct(x.shape, _F32),
            jax.ShapeDtypeStruct((n, HIST_A, SUBLANES, D_A), _F32),
            jax.ShapeDtypeStruct((n, HIST_3, SUBLANES, D_B), _F32),
            jax.ShapeDtypeStruct((n, HIST_3, SUBLANES, D_FF), _F32),
        ],
        scratch_shapes=[
            pltpu.VMEM((t, D_MODEL), _BF16),
            pltpu.VMEM((n_lt, HIST_A + groups, SUBLANES, LANES), _F32),
            pltpu.VMEM((n_lt, groups, SUBLANES, LANES), _F32),
            pltpu.VMEM((GATE_CHUNKS, t, GATE_W), _F32),
            pltpu.VMEM((HIST_3, SUBLANES, D_B), _F32),
            pltpu.VMEM((HIST_3, SUBLANES, D_FF), _F32),
        ],
        compiler_params=pltpu.CompilerParams(
            dimension_semantics=("arbitrary", "arbitrary"), vmem_limit_bytes=VMEM_LIMIT_BYTES),
        name="layer",
    )(x, ha, hb, hf, p["norm1_g"], p["w_in"], p["b_gate"], p["dw_a"], p["b_dw_a"],
      p["ln_a_g"], p["ln_a_b"], p["w_a_out"], p["dw_b"], p["w_b_out"], p["w_o"], p["norm2_g"], p["w_up"],
      p["dw_f"], p["w_down"], p["final_g"])


class _Layout:
    def __init__(self, n, s, time_strided):
        self.n, self.s, self.time_strided = n, s, time_strided
        if time_strided:
            self.t = min(s, BLOCK_ROWS)
            self.blocks = n
            assert s % self.t == 0
        else:
            self.t = min(SUBLANES * s, BLOCK_ROWS)
            self.blocks = n // SUBLANES
            assert n % SUBLANES == 0 and (SUBLANES * s) % self.t == 0
        self.p = self.t // SUBLANES
        assert self.p >= HIST_A and self.p % CONV_GROUPS == 0

    def to_rows(self, x):
        c = x.shape[-1]
        if self.time_strided:
            return x
        y = x.reshape(self.blocks, SUBLANES, self.s, c).transpose(0, 2, 1, 3)
        return y.reshape(self.blocks, SUBLANES * self.s, c)

    def from_rows(self, y):
        c = y.shape[-1]
        if self.time_strided:
            return y
        x = y.reshape(self.blocks, self.s, SUBLANES, c).transpose(0, 2, 1, 3)
        return x.reshape(self.n, self.s, c)

    def state_in(self, cache, groups):
        n, w, c = cache.shape
        if self.time_strided:
            st = jnp.zeros((n, groups, SUBLANES, c), cache.dtype)
            return st.at[:, groups - w:, SUBLANES - 1, :].set(cache)
        st = cache.reshape(self.blocks, SUBLANES, w, c).transpose(0, 2, 1, 3)
        return jnp.pad(st, ((0, 0), (groups - w, 0), (0, 0), (0, 0)))

    def state_out(self, tail, w):
        groups, c = tail.shape[1], tail.shape[-1]
        if self.time_strided:
            return tail[:, groups - w:, SUBLANES - 1, :]
        return tail[:, groups - w:].transpose(0, 2, 1, 3).reshape(self.n, w, c)


def kernel(x_prompt, x_sample, cache_conv_a, cache_conv_b, cache_ffn_conv, norm1_g, w_in, b_gate,
           dw_a, b_dw_a, ln_a_g, ln_a_b, w_a_out, dw_b, w_b_out, w_o, norm2_g, w_up, dw_f,
           w_down, final_g):
    depth = w_in.shape[0]
    n_p = x_prompt.shape[0]
    row = lambda a: a.reshape(a.shape[0], 1, a.shape[1])
    p = dict(
        norm1_g=row(norm1_g), w_in=w_in.astype(_BF16), b_gate=row(b_gate),
        dw_a=dw_a.reshape(depth, CONV_A, D_A // LANES, LANES).transpose(0, 2, 1, 3),
        b_dw_a=row(b_dw_a), ln_a_g=row(ln_a_g), ln_a_b=row(ln_a_b), w_a_out=w_a_out.astype(_BF16),
        dw_b=dw_b, w_b_out=w_b_out.astype(_BF16), w_o=w_o.astype(_BF16), norm2_g=row(norm2_g),
        w_up=w_up.astype(_BF16), dw_f=dw_f,
        w_down=w_down.astype(_BF16), final_g=final_g.reshape(1, D_MODEL),
    )
    groups = [
        (_Layout(n_p, x_prompt.shape[1], True), x_prompt,
         lambda l: (jnp.zeros((n_p, CONV_A - 1, D_A), _F32), jnp.zeros((n_p, CONV_B - 1, D_B), _F32),
                    jnp.zeros((n_p, CONV_F - 1, D_FF), _F32))),
        (_Layout(x_sample.shape[0], x_sample.shape[1], False), x_sample,
         lambda l: (cache_conv_a[l], cache_conv_b[l], cache_ffn_conv[l])),
    ]
    ys, new_a, new_b, new_f = [], [], [], []
    for lay, x, caches in groups:
        xr = lay.to_rows(x)
        sa, sb, sf = [], [], []
        for l in range(depth):
            ca, cb, cf = caches(l)
            xr, ta, tb, tf = _layer(xr, lay.state_in(ca, HIST_A), lay.state_in(cb, HIST_3),
                                    lay.state_in(cf, HIST_3), l, p, lay.t, lay.time_strided,
                                    lay.time_strided and l == 0, lay.time_strided and l == depth - 1,
                                    l == depth - 1)
            sa.append(lay.state_out(ta, CONV_A - 1))
            sb.append(lay.state_out(tb, CONV_B - 1))
            sf.append(lay.state_out(tf, CONV_F - 1))
        ys.append(lay.from_rows(xr))
        new_a.append(jnp.stack(sa)); new_b.append(jnp.stack(sb)); new_f.append(jnp.stack(sf))
    return (ys[0], ys[1], new_a[0], new_b[0], new_f[0], new_a[1], new_b[1], new_f[1])
```

```python
import functools

import jax
import jax.numpy as jnp
from jax import lax
from jax.experimental import pallas as pl
from jax.experimental.pallas import tpu as pltpu

D_MODEL = 1024
D_A = 512
D_B = 512
CONV_A = 31
CONV_B = 3
D_FF = 2816
CONV_F = 3
RMS_EPS = 1e-6
LN_EPS = 1e-5

SUBLANES = 8
LANES = 128
HIST_A = 32
HIST_3 = CONV_B - 1
CONV_GROUPS = 8
FF_CHUNK = 256
GATE_CHUNKS = 4
GATES_PER_TRIP = 1
GATE_W = 2 * D_MODEL // GATE_CHUNKS
GATE_COL0 = 2 * D_A + 3 * D_B
D_IN = GATE_COL0 + 2 * D_MODEL
BLOCK_ROWS = 512
VMEM_LIMIT_BYTES = 60 * 1024 * 1024

_F32 = jnp.float32
_BF16 = jnp.bfloat16


def _rms_norm(x, g):
    y = x * lax.rsqrt(jnp.mean(x * x, axis=-1, keepdims=True) + RMS_EPS)
    return y * g


def _dot(a, b):
    return jnp.dot(a, b, preferred_element_type=_F32)


def _history(prev_tail, tail, time_strided):
    if not time_strided:
        return prev_tail
    sub = lax.broadcasted_iota(jnp.int32, tail.shape, 1)
    return jnp.where(sub == 0, pltpu.roll(prev_tail, 1, axis=1), pltpu.roll(tail, 1, axis=1))


def _conv3(v, prev_ref, w_ref, c0, c1, time_strided):
    p = v.shape[0]
    tail = v[p - HIST_3:]
    hist = _history(prev_ref[:, :, c0:c1], tail, time_strided)
    prev_ref[:, :, c0:c1] = tail
    e = jnp.concatenate([hist, v], axis=0)
    out = e[0:p] * w_ref[0:1, c0:c1][None]
    for k in range(1, CONV_B):
        out = out + e[k:k + p] * w_ref[k:k + 1, c0:c1][None]
    return out


def _layer_kernel(x_ref, ha_ref, hb_ref, hf_ref,
                  n1g_ref, win_ref, bgate_ref, dwa_ref, bdwa_ref, lng_ref, lnb_ref, wa_ref,
                  dwb_ref, wb_ref, wo_ref, n2g_ref, wup_ref, dwf_ref, wdn_ref, fg_ref,
                  xo_ref, ta_ref, tb_ref, tf_ref,
                  h_ref, e_ref, c_ref, g_ref, pb_ref, pf_ref, *, t, time_strided, permute_in, permute_out,
                  final):
    ti = pl.program_id(1)
    p = t // SUBLANES
    n_lt = D_A // LANES

    @pl.when(ti == 0)
    def _():
        for j in range(n_lt):
            e_ref[j, p:p + HIST_A] = ha_ref[:, :, j * LANES:(j + 1) * LANES]
        pb_ref[...] = hb_ref[...]
        pf_ref[...] = hf_ref[...]

    x = x_ref[...]
    if permute_in:
        x = pltpu.einshape("sgd->gsd", x.reshape(SUBLANES, p, D_MODEL)).reshape(t, D_MODEL)
    h = _rms_norm(x, n1g_ref[...]).astype(_BF16)
    h_ref[...] = h
    za = _dot(h, win_ref[:, 0:2 * D_A])
    zb = _dot(h, win_ref[:, 2 * D_A:GATE_COL0])

    u = (za[:, :D_A] * jax.nn.sigmoid(za[:, D_A:])).reshape(p, SUBLANES, D_A)
    tail = u[p - HIST_A:]
    ta_ref[...] = tail
    for j in range(n_lt):
        lanes = slice(j * LANES, (j + 1) * LANES)
        e_ref[j, 0:HIST_A] = _history(e_ref[j, p:p + HIST_A], tail[:, :, lanes], time_strided)
        e_ref[j, HIST_A:HIST_A + p] = u[:, :, lanes]

    cv = (zb[:, D_B:2 * D_B] * zb[:, 2 * D_B:]).reshape(p, SUBLANES, D_B)
    tb_ref[...] = cv[p - HIST_3:]
    v = _conv3(cv, pb_ref, dwb_ref, 0, D_B, time_strided).reshape(t, D_B)
    yb_in = (zb[:, :D_B] * v).astype(_BF16)

    tap0 = HIST_A - (CONV_A - 1)
    n_gb = p // CONV_GROUPS
    blocks_per_trip = n_lt * n_gb * GATES_PER_TRIP // GATE_CHUNKS

    def gate_chunk(c, after):
        lhs = h_ref[...]
        if after is not None:
            zero = (pltpu.bitcast(after[0], jnp.uint32) >> 16) >> 16
            top = pltpu.bitcast(pltpu.bitcast(lhs[0:16, 0:LANES], jnp.uint32) | zero, _BF16)
            lhs = jnp.concatenate(
                [jnp.concatenate([top, lhs[0:16, LANES:]], axis=1), lhs[16:]], axis=0)
        col = pl.multiple_of(c * GATE_W, GATE_W)
        g_ref[c] = _dot(lhs, win_ref[:, pl.ds(GATE_COL0 + col, GATE_W)]) + bgate_ref[:, pl.ds(col, GATE_W)]

    def conv_block(blk):
        j = blk // n_gb
        g0 = pl.multiple_of((blk % n_gb) * CONV_GROUPS, CONV_GROUPS)
        win = e_ref[j, pl.ds(g0, CONV_GROUPS + HIST_A)]
        acc = win[tap0:tap0 + CONV_GROUPS] * dwa_ref[j, 0:1, :][None]
        for k in range(1, CONV_A):
            acc = acc + win[tap0 + k:tap0 + k + CONV_GROUPS] * dwa_ref[j, k:k + 1, :][None]
        c_ref[j, pl.ds(g0, CONV_GROUPS)] = acc
        return acc

    blocks_per_gate = blocks_per_trip // GATES_PER_TRIP

    def conv_body(i, carry):
        after = None
        for q in range(GATES_PER_TRIP):
            gate_chunk(i * GATES_PER_TRIP + q, after)
            for r in range(blocks_per_gate):
                after = conv_block(i * blocks_per_trip + q * blocks_per_gate + r)
        return carry

    lax.fori_loop(0, GATE_CHUNKS // GATES_PER_TRIP, conv_body, 0)

    y_b = _dot(yb_in, wb_ref[...])
    conv = jnp.concatenate([c_ref[j] for j in range(n_lt)], axis=-1).reshape(t, D_A) + bdwa_ref[...]
    mu = jnp.mean(conv, axis=-1, keepdims=True)
    xc = conv - mu
    var = jnp.mean(xc * xc, axis=-1, keepdims=True)
    ya = xc * lax.rsqrt(var + LN_EPS) * lng_ref[...] + lnb_ref[...]
    y_a = _dot((ya * jax.nn.sigmoid(ya)).astype(_BF16), wa_ref[...])

    half = GATE_CHUNKS // 2
    g_a = jax.nn.sigmoid(jnp.concatenate([g_ref[c] for c in range(half)], axis=-1))
    g_b = jax.nn.sigmoid(jnp.concatenate([g_ref[c] for c in range(half, GATE_CHUNKS)], axis=-1))
    mix = (g_a * y_a + g_b * y_b).astype(_BF16)
    x1 = x + _dot(mix, wo_ref[...])

    h2 = _rms_norm(x1, n2g_ref[...]).astype(_BF16)
    acc = x1
    def up(c0):
        w = jnp.concatenate([wup_ref[:, c0:c0 + FF_CHUNK], wup_ref[:, D_FF + c0:D_FF + c0 + FF_CHUNK]], axis=1)
        return _dot(h2, w)

    gv_next = up(0)
    for c0 in range(0, D_FF, FF_CHUNK):
        c1 = c0 + FF_CHUNK
        gv = gv_next
        if c1 < D_FF:
            gv_next = up(c1)
        gate = gv[:, :FF_CHUNK].reshape(p, SUBLANES, FF_CHUNK)
        tf_ref[:, :, c0:c1] = gate[p - HIST_3:]
        gc = _conv3(gate, pf_ref, dwf_ref, c0, c1, time_strided).reshape(t, FF_CHUNK)
        hh = (gc * jax.nn.sigmoid(gc) * gv[:, FF_CHUNK:]).astype(_BF16)
        acc = acc + _dot(hh, wdn_ref[c0:c1, :])
    if final:
        acc = _rms_norm(acc, fg_ref[...])
    if permute_out:
        acc = pltpu.einshape("gsd->sgd", acc.reshape(p, SUBLANES, D_MODEL)).reshape(t, D_MODEL)
    xo_ref[...] = acc


def _resident(shape, layer):
    zeros = (0,) * len(shape)
    return pl.BlockSpec((None,) + shape, lambda b, s: (layer,) + zeros, pipeline_mode=pl.Buffered(1))


def _state_spec(groups, ch):
    return pl.BlockSpec((None, groups, SUBLANES, ch), lambda b, i: (b, 0, 0, 0))


def _layer(x, ha, hb, hf, layer, p, t, time_strided, permute_in, permute_out, final):
    n, s, _ = x.shape
    x_spec = pl.BlockSpec((None, t, D_MODEL), lambda b, i: (b, i, 0))
    groups = t // SUBLANES
    n_lt = D_A // LANES
    return pl.pallas_call(
        functools.partial(_layer_kernel, t=t, time_strided=time_strided, permute_in=permute_in,
                          permute_out=permute_out, final=final),
        grid=(n, s // t),
        in_specs=[
            x_spec, _state_spec(HIST_A, D_A), _state_spec(HIST_3, D_B), _state_spec(HIST_3, D_FF),
            _resident((1, D_MODEL), layer), _resident((D_MODEL, D_IN), layer),
            _resident((1, 2 * D_MODEL), layer),
            _resident((n_lt, CONV_A, LANES), layer),
            _resident((1, D_A), layer), _resident((1, D_A), layer), _resident((1, D_A), layer),
            _resident((D_A, D_MODEL), layer), _resident((CONV_B, D_B), layer),
            _resident((D_B, D_MODEL), layer), _resident((D_MODEL, D_MODEL), layer),
            _resident((1, D_MODEL), layer), _resident((D_MODEL, 2 * D_FF), layer),
            _resident((CONV_F, D_FF), layer), _resident((D_FF, D_MODEL), layer),
            pl.BlockSpec((1, D_MODEL), lambda b, i: (0, 0)),
        ],
        out_specs=[x_spec, _state_spec(HIST_A, D_A), _state_spec(HIST_3, D_B), _state_spec(HIST_3, D_FF)],
        out_shape=[
            jax.ShapeDtypeStruct(x.shape, _F32),
            jax.ShapeDtypeStruct((n, HIST_A, SUBLANES, D_A), _F32),
            jax.ShapeDtypeStruct((n, HIST_3, SUBLANES, D_B), _F32),
            jax.ShapeDtypeStruct((n, HIST_3, SUBLANES, D_FF), _F32),
        ],
        scratch_shapes=[
            pltpu.VMEM((t, D_MODEL), _BF16),
            pltpu.VMEM((n_lt, HIST_A + groups, SUBLANES, LANES), _F32),
            pltpu.VMEM((n_lt, groups, SUBLANES, LANES), _F32),
            pltpu.VMEM((GATE_CHUNKS, t, GATE_W), _F32),
            pltpu.VMEM((HIST_3, SUBLANES, D_B), _F32),
            pltpu.VMEM((HIST_3, SUBLANES, D_FF), _F32),
        ],
        compiler_params=pltpu.CompilerParams(
            dimension_semantics=("arbitrary", "arbitrary"), vmem_limit_bytes=VMEM_LIMIT_BYTES),
        name="layer",
    )(x, ha, hb, hf, p["norm1_g"], p["w_in"], p["b_gate"], p["dw_a"], p["b_dw_a"],
      p["ln_a_g"], p["ln_a_b"], p["w_a_out"], p["dw_b"], p["w_b_out"], p["w_o"], p["norm2_g"], p["w_up"],
      p["dw_f"], p["w_down"], p["final_g"])


class _Layout:
    def __init__(self, n, s, time_strided):
        self.n, self.s, self.time_strided = n, s, time_strided
        if time_strided:
            self.t = min(s, BLOCK_ROWS)
            self.blocks = n
            assert s % self.t == 0
        else:
            self.t = min(SUBLANES * s, BLOCK_ROWS)
            self.blocks = n // SUBLANES
            assert n % SUBLANES == 0 and (SUBLANES * s) % self.t == 0
        self.p = self.t // SUBLANES
        assert self.p >= HIST_A and self.p % CONV_GROUPS == 0

    def to_rows(self, x):
        c = x.shape[-1]
        if self.time_strided:
            return x
        y = x.reshape(self.blocks, SUBLANES, self.s, c).transpose(0, 2, 1, 3)
        return y.reshape(self.blocks, SUBLANES * self.s, c)

    def from_rows(self, y):
        c = y.shape[-1]
        if self.time_strided:
            return y
        x = y.reshape(self.blocks, self.s, SUBLANES, c).transpose(0, 2, 1, 3)
        return x.reshape(self.n, self.s, c)

    def state_in(self, cache, groups):
        n, w, c = cache.shape
        if self.time_strided:
            st = jnp.zeros((n, groups, SUBLANES, c), cache.dtype)
            return st.at[:, groups - w:, SUBLANES - 1, :].set(cache)
        st = cache.reshape(self.blocks, SUBLANES, w, c).transpose(0, 2, 1, 3)
        return jnp.pad(st, ((0, 0), (groups - w, 0), (0, 0), (0, 0)))

    def state_out(self, tail, w):
        groups, c = tail.shape[1], tail.shape[-1]
        if self.time_strided:
            return tail[:, groups - w:, SUBLANES - 1, :]
        return tail[:, groups - w:].transpose(0, 2, 1, 3).reshape(self.n, w, c)


def kernel(x_prompt, x_sample, cache_conv_a, cache_conv_b, cache_ffn_conv, norm1_g, w_in, b_gate,
           dw_a, b_dw_a, ln_a_g, ln_a_b, w_a_out, dw_b, w_b_out, w_o, norm2_g, w_up, dw_f,
           w_down, final_g):
    depth = w_in.shape[0]
    n_p = x_prompt.shape[0]
    row = lambda a: a.reshape(a.shape[0], 1, a.shape[1])
    p = dict(
        norm1_g=row(norm1_g), w_in=w_in.astype(_BF16), b_gate=row(b_gate),
        dw_a=dw_a.reshape(depth, CONV_A, D_A // LANES, LANES).transpose(0, 2, 1, 3),
        b_dw_a=row(b_dw_a), ln_a_g=row(ln_a_g), ln_a_b=row(ln_a_b), w_a_out=w_a_out.astype(_BF16),
        dw_b=dw_b, w_b_out=w_b_out.astype(_BF16), w_o=w_o.astype(_BF16), norm2_g=row(norm2_g),
        w_up=w_up.astype(_BF16), dw_f=dw_f,
        w_down=w_down.astype(_BF16), final_g=final_g.reshape(1, D_MODEL),
    )
    groups = [
        (_Layout(n_p, x_prompt.shape[1], True), x_prompt,
         lambda l: (jnp.zeros((n_p, CONV_A - 1, D_A), _F32), jnp.zeros((n_p, CONV_B - 1, D_B), _F32),
                    jnp.zeros((n_p, CONV_F - 1, D_FF), _F32))),
        (_Layout(x_sample.shape[0], x_sample.shape[1], False), x_sample,
         lambda l: (cache_conv_a[l], cache_conv_b[l], cache_ffn_conv[l])),
    ]
    ys, new_a, new_b, new_f = [], [], [], []
    for lay, x, caches in groups:
        xr = lay.to_rows(x)
        sa, sb, sf = [], [], []
        for l in range(depth):
            ca, cb, cf = caches(l)
            xr, ta, tb, tf = _layer(xr, lay.state_in(ca, HIST_A), lay.state_in(cb, HIST_3),
                                    lay.state_in(cf, HIST_3), l, p, lay.t, lay.time_strided,
                                    lay.time_strided and l == 0, lay.time_strided and l == depth - 1,
                                    l == depth - 1)
            sa.append(lay.state_out(ta, CONV_A - 1))
            sb.append(lay.state_out(tb, CONV_B - 1))
            sf.append(lay.state_out(tf, CONV_F - 1))
        ys.append(lay.from_rows(xr))
        new_a.append(jnp.stack(sa)); new_b.append(jnp.stack(sb)); new_f.append(jnp.stack(sf))
    return (ys[0], ys[1], new_a[0], new_b[0], new_f[0], new_a[1], new_b[1], new_f[1])
```

```python
import functools

import jax
import jax.numpy as jnp
from jax import lax
from jax.experimental import pallas as pl
from jax.experimental.pallas import tpu as pltpu

D_MODEL = 1024
D_A = 512
D_B = 512
CONV_A = 31
CONV_B = 3
D_FF = 2816
CONV_F = 3
RMS_EPS = 1e-6
LN_EPS = 1e-5

SUBLANES = 8
LANES = 128
HIST_A = 32
HIST_3 = CONV_B - 1
CONV_GROUPS = 8
FF_CHUNK = 256
GATE_CHUNKS = 4
GATE_W = 2 * D_MODEL // GATE_CHUNKS
GATE_COL0 = 2 * D_A + 3 * D_B
D_IN = GATE_COL0 + 2 * D_MODEL
BLOCK_ROWS = 256
VMEM_LIMIT_BYTES = 56 * 1024 * 1024

_F32 = jnp.float32
_BF16 = jnp.bfloat16


def _rms_norm(x, g):
    y = x * lax.rsqrt(jnp.mean(x * x, axis=-1, keepdims=True) + RMS_EPS)
    return y * g


def _dot(a, b):
    return jnp.dot(a, b, preferred_element_type=_F32)


def _history(prev_tail, tail, time_strided):
    if not time_strided:
        return prev_tail
    sub = lax.broadcasted_iota(jnp.int32, tail.shape, 1)
    return jnp.where(sub == 0, pltpu.roll(prev_tail, 1, axis=1), pltpu.roll(tail, 1, axis=1))


def _conv3(v, prev_ref, w_ref, c0, c1, time_strided):
    p = v.shape[0]
    tail = v[p - HIST_3:]
    hist = _history(prev_ref[:, :, c0:c1], tail, time_strided)
    prev_ref[:, :, c0:c1] = tail
    e = jnp.concatenate([hist, v], axis=0)
    out = e[0:p] * w_ref[0:1, c0:c1][None]
    for k in range(1, CONV_B):
        out = out + e[k:k + p] * w_ref[k:k + 1, c0:c1][None]
    return out


def _layer_kernel(x_ref, ha_ref, hb_ref, hf_ref,
                  n1g_ref, win_ref, bgate_ref, dwa_ref, bdwa_ref, lng_ref, lnb_ref, wa_ref,
                  dwb_ref, wb_ref, wo_ref, n2g_ref, wup_ref, dwf_ref, wdn_ref, fg_ref,
                  xo_ref, ta_ref, tb_ref, tf_ref,
                  h_ref, e_ref, c_ref, g_ref, pb_ref, pf_ref, *, t, time_strided, permute_in, permute_out,
                  final):
    ti = pl.program_id(1)
    p = t // SUBLANES
    n_lt = D_A // LANES

    @pl.when(ti == 0)
    def _():
        for j in range(n_lt):
            e_ref[j, p:p + HIST_A] = ha_ref[:, :, j * LANES:(j + 1) * LANES]
        pb_ref[...] = hb_ref[...]
        pf_ref[...] = hf_ref[...]

    x = x_ref[...]
    if permute_in:
        x = pltpu.einshape("sgd->gsd", x.reshape(SUBLANES, p, D_MODEL)).reshape(t, D_MODEL)
    h_ref[...] = _rms_norm(x, n1g_ref[...]).astype(_BF16)
    za = _dot(h_ref[...], win_ref[:, 0:2 * D_A])
    zb = _dot(h_ref[...], win_ref[:, 2 * D_A:GATE_COL0])

    u = (za[:, :D_A] * jax.nn.sigmoid(za[:, D_A:])).reshape(p, SUBLANES, D_A)
    tail = u[p - HIST_A:]
    ta_ref[...] = tail
    for j in range(n_lt):
        lanes = slice(j * LANES, (j + 1) * LANES)
        e_ref[j, 0:HIST_A] = _history(e_ref[j, p:p + HIST_A], tail[:, :, lanes], time_strided)
        e_ref[j, HIST_A:HIST_A + p] = u[:, :, lanes]

    cv = (zb[:, D_B:2 * D_B] * zb[:, 2 * D_B:]).reshape(p, SUBLANES, D_B)
    tb_ref[...] = cv[p - HIST_3:]
    v = _conv3(cv, pb_ref, dwb_ref, 0, D_B, time_strided).reshape(t, D_B)
    yb_in = (zb[:, :D_B] * v).astype(_BF16)

    tap0 = HIST_A - (CONV_A - 1)
    n_gb = p // CONV_GROUPS
    blocks_per_trip = n_lt * n_gb // GATE_CHUNKS

    def conv_body(i, carry):
        col = pl.multiple_of(i * GATE_W, GATE_W)
        g_ref[i] = (_dot(h_ref[...], win_ref[:, pl.ds(GATE_COL0 + col, GATE_W)])
                    + bgate_ref[:, pl.ds(col, GATE_W)])
        for q in range(blocks_per_trip):
            blk = i * blocks_per_trip + q
            j = blk // n_gb
            g0 = pl.multiple_of((blk % n_gb) * CONV_GROUPS, CONV_GROUPS)
            win = e_ref[j, pl.ds(g0, CONV_GROUPS + HIST_A)]
            acc = win[tap0:tap0 + CONV_GROUPS] * dwa_ref[j, 0:1, :][None]
            for k in range(1, CONV_A):
                acc = acc + win[tap0 + k:tap0 + k + CONV_GROUPS] * dwa_ref[j, k:k + 1, :][None]
            c_ref[j, pl.ds(g0, CONV_GROUPS)] = acc
        return carry

    lax.fori_loop(0, GATE_CHUNKS, conv_body, 0)

    y_b = _dot(yb_in, wb_ref[...])
    conv = jnp.concatenate([c_ref[j] for j in range(n_lt)], axis=-1).reshape(t, D_A) + bdwa_ref[...]
    mu = jnp.mean(conv, axis=-1, keepdims=True)
    xc = conv - mu
    var = jnp.mean(xc * xc, axis=-1, keepdims=True)
    ya = xc * lax.rsqrt(var + LN_EPS) * lng_ref[...] + lnb_ref[...]
    y_a = _dot((ya * jax.nn.sigmoid(ya)).astype(_BF16), wa_ref[...])

    half = GATE_CHUNKS // 2
    g_a = jax.nn.sigmoid(jnp.concatenate([g_ref[c] for c in range(half)], axis=-1))
    g_b = jax.nn.sigmoid(jnp.concatenate([g_ref[c] for c in range(half, GATE_CHUNKS)], axis=-1))
    mix = (g_a * y_a + g_b * y_b).astype(_BF16)
    x1 = (x if permute_in else x_ref[...]) + _dot(mix, wo_ref[...])

    h2 = _rms_norm(x1, n2g_ref[...]).astype(_BF16)
    acc = x1
    def up(c0):
        w = jnp.concatenate([wup_ref[:, c0:c0 + FF_CHUNK], wup_ref[:, D_FF + c0:D_FF + c0 + FF_CHUNK]], axis=1)
        return _dot(h2, w)

    gv_next = up(0)
    for c0 in range(0, D_FF, FF_CHUNK):
        c1 = c0 + FF_CHUNK
        gv = gv_next
        if c1 < D_FF:
            gv_next = up(c1)
        gate = gv[:, :FF_CHUNK].reshape(p, SUBLANES, FF_CHUNK)
        tf_ref[:, :, c0:c1] = gate[p - HIST_3:]
        gc = _conv3(gate, pf_ref, dwf_ref, c0, c1, time_strided).reshape(t, FF_CHUNK)
        hh = (gc * jax.nn.sigmoid(gc) * gv[:, FF_CHUNK:]).astype(_BF16)
        acc = acc + _dot(hh, wdn_ref[c0:c1, :])
    if final:
        acc = _rms_norm(acc, fg_ref[...])
    if permute_out:
        acc = pltpu.einshape("gsd->sgd", acc.reshape(p, SUBLANES, D_MODEL)).reshape(t, D_MODEL)
    xo_ref[...] = acc


def _resident(shape, layer):
    zeros = (0,) * len(shape)
    return pl.BlockSpec((None,) + shape, lambda b, s: (layer,) + zeros, pipeline_mode=pl.Buffered(1))


def _state_spec(groups, ch):
    return pl.BlockSpec((None, groups, SUBLANES, ch), lambda b, i: (b, 0, 0, 0))


def _layer(x, ha, hb, hf, layer, p, t, time_strided, permute_in, permute_out, final):
    n, s, _ = x.shape
    x_spec = pl.BlockSpec((None, t, D_MODEL), lambda b, i: (b, i, 0))
    groups = t // SUBLANES
    n_lt = D_A // LANES
    return pl.pallas_call(
        functools.partial(_layer_kernel, t=t, time_strided=time_strided, permute_in=permute_in,
                          permute_out=permute_out, final=final),
        grid=(n, s // t),
        in_specs=[
            x_spec, _state_spec(HIST_A, D_A), _state_spec(HIST_3, D_B), _state_spec(HIST_3, D_FF),
            _resident((1, D_MODEL), layer), _resident((D_MODEL, D_IN), layer),
            _resident((1, 2 * D_MODEL), layer),
            _resident((n_lt, CONV_A, LANES), layer),
            _resident((1, D_A), layer), _resident((1, D_A), layer), _resident((1, D_A), layer),
            _resident((D_A, D_MODEL), layer), _resident((CONV_B, D_B), layer),
            _resident((D_B, D_MODEL), layer), _resident((D_MODEL, D_MODEL), layer),
            _resident((1, D_MODEL), layer), _resident((D_MODEL, 2 * D_FF), layer),
            _resident((CONV_F, D_FF), layer), _resident((D_FF, D_MODEL), layer),
            pl.BlockSpec((1, D_MODEL), lambda b, i: (0, 0)),
        ],
        out_specs=[x_spec, _state_spec(HIST_A, D_A), _state_spec(HIST_3, D_B), _state_spec(HIST_3, D_FF)],
        out_shape=[
            jax.ShapeDtypeStruct(x.shape, _F32),
            jax.ShapeDtypeStruct((n, HIST_A, SUBLANES, D_A), _F32),
            jax.ShapeDtypeStruct((n, HIST_3, SUBLANES, D_B), _F32),
            jax.ShapeDtypeStruct((n, HIST_3, SUBLANES, D_FF), _F32),
        ],
        scratch_shapes=[
            pltpu.VMEM((t, D_MODEL), _BF16),
            pltpu.VMEM((n_lt, HIST_A + groups, SUBLANES, LANES), _F32),
            pltpu.VMEM((n_lt, groups, SUBLANES, LANES), _F32),
            pltpu.VMEM((GATE_CHUNKS, t, GATE_W), _F32),
            pltpu.VMEM((HIST_3, SUBLANES, D_B), _F32),
            pltpu.VMEM((HIST_3, SUBLANES, D_FF), _F32),
        ],
        compiler_params=pltpu.CompilerParams(
            dimension_semantics=("arbitrary", "arbitrary"), vmem_limit_bytes=VMEM_LIMIT_BYTES),
        name="layer",
    )(x, ha, hb, hf, p["norm1_g"], p["w_in"], p["b_gate"], p["dw_a"], p["b_dw_a"],
      p["ln_a_g"], p["ln_a_b"], p["w_a_out"], p["dw_b"], p["w_b_out"], p["w_o"], p["norm2_g"], p["w_up"],
      p["dw_f"], p["w_down"], p["final_g"])


class _Layout:
    def __init__(self, n, s, time_strided):
        self.n, self.s, self.time_strided = n, s, time_strided
        if time_strided:
            self.t = min(s, BLOCK_ROWS)
            self.blocks = n
            assert s % self.t == 0
        else:
            self.t = min(SUBLANES * s, BLOCK_ROWS)
            self.blocks = n // SUBLANES
            assert n % SUBLANES == 0 and (SUBLANES * s) % self.t == 0
        self.p = self.t // SUBLANES
        assert self.p >= HIST_A and self.p % CONV_GROUPS == 0

    def to_rows(self, x):
        c = x.shape[-1]
        if self.time_strided:
            return x
        y = x.reshape(self.blocks, SUBLANES, self.s, c).transpose(0, 2, 1, 3)
        return y.reshape(self.blocks, SUBLANES * self.s, c)

    def from_rows(self, y):
        c = y.shape[-1]
        if self.time_strided:
            return y
        x = y.reshape(self.blocks, self.s, SUBLANES, c).transpose(0, 2, 1, 3)
        return x.reshape(self.n, self.s, c)

    def state_in(self, cache, groups):
        n, w, c = cache.shape
        if self.time_strided:
            st = jnp.zeros((n, groups, SUBLANES, c), cache.dtype)
            return st.at[:, groups - w:, SUBLANES - 1, :].set(cache)
        st = cache.reshape(self.blocks, SUBLANES, w, c).transpose(0, 2, 1, 3)
        return jnp.pad(st, ((0, 0), (groups - w, 0), (0, 0), (0, 0)))

    def state_out(self, tail, w):
        groups, c = tail.shape[1], tail.shape[-1]
        if self.time_strided:
            return tail[:, groups - w:, SUBLANES - 1, :]
        return tail[:, groups - w:].transpose(0, 2, 1, 3).reshape(self.n, w, c)


def kernel(x_prompt, x_sample, cache_conv_a, cache_conv_b, cache_ffn_conv, norm1_g, w_in, b_gate,
           dw_a, b_dw_a, ln_a_g, ln_a_b, w_a_out, dw_b, w_b_out, w_o, norm2_g, w_up, dw_f,
           w_down, final_g):
    depth = w_in.shape[0]
    n_p = x_prompt.shape[0]
    row = lambda a: a.reshape(a.shape[0], 1, a.shape[1])
    p = dict(
        norm1_g=row(norm1_g), w_in=w_in.astype(_BF16), b_gate=row(b_gate),
        dw_a=dw_a.reshape(depth, CONV_A, D_A // LANES, LANES).transpose(0, 2, 1, 3),
        b_dw_a=row(b_dw_a), ln_a_g=row(ln_a_g), ln_a_b=row(ln_a_b), w_a_out=w_a_out.astype(_BF16),
        dw_b=dw_b, w_b_out=w_b_out.astype(_BF16), w_o=w_o.astype(_BF16), norm2_g=row(norm2_g),
        w_up=w_up.astype(_BF16), dw_f=dw_f,
        w_down=w_down.astype(_BF16), final_g=final_g.reshape(1, D_MODEL),
    )
    groups = [
        (_Layout(n_p, x_prompt.shape[1], True), x_prompt,
         lambda l: (jnp.zeros((n_p, CONV_A - 1, D_A), _F32), jnp.zeros((n_p, CONV_B - 1, D_B), _F32),
                    jnp.zeros((n_p, CONV_F - 1, D_FF), _F32))),
        (_Layout(x_sample.shape[0], x_sample.shape[1], False), x_sample,
         lambda l: (cache_conv_a[l], cache_conv_b[l], cache_ffn_conv[l])),
    ]
    ys, new_a, new_b, new_f = [], [], [], []
    for lay, x, caches in groups:
        xr = lay.to_rows(x)
        sa, sb, sf = [], [], []
        for l in range(depth):
            ca, cb, cf = caches(l)
            xr, ta, tb, tf = _layer(xr, lay.state_in(ca, HIST_A), lay.state_in(cb, HIST_3),
                                    lay.state_in(cf, HIST_3), l, p, lay.t, lay.time_strided,
                                    lay.time_strided and l == 0, lay.time_strided and l == depth - 1,
                                    l == depth - 1)
            sa.append(lay.state_out(ta, CONV_A - 1))
            sb.append(lay.state_out(tb, CONV_B - 1))
            sf.append(lay.state_out(tf, CONV_F - 1))
        ys.append(lay.from_rows(xr))
        new_a.append(jnp.stack(sa)); new_b.append(jnp.stack(sb)); new_f.append(jnp.stack(sf))
    return (ys[0], ys[1], new_a[0], new_b[0], new_f[0], new_a[1], new_b[1], new_f[1])
```

```python
import functools

import jax
import jax.numpy as jnp
from jax import lax
from jax.experimental import pallas as pl
from jax.experimental.pallas import tpu as pltpu

D_MODEL = 1024
D_A = 512
D_B = 512
CONV_A = 31
CONV_B = 3
D_FF = 2816
CONV_F = 3
RMS_EPS = 1e-6
LN_EPS = 1e-5

SUBLANES = 8
LANES = 128
HIST_A = 32
HIST_3 = CONV_B - 1
CONV_GROUPS = 8
FF_CHUNK = 256
GATE_CHUNKS = 8
GATES_PER_TRIP = 4
GATE_W = 2 * D_MODEL // GATE_CHUNKS
GATE_COL0 = 2 * D_A + 3 * D_B
D_IN = GATE_COL0 + 2 * D_MODEL
BLOCK_ROWS = 256
VMEM_LIMIT_BYTES = 56 * 1024 * 1024

_F32 = jnp.float32
_BF16 = jnp.bfloat16


def _rms_norm(x, g):
    y = x * lax.rsqrt(jnp.mean(x * x, axis=-1, keepdims=True) + RMS_EPS)
    return y * g


def _dot(a, b):
    return jnp.dot(a, b, preferred_element_type=_F32)


def _history(prev_tail, tail, time_strided):
    if not time_strided:
        return prev_tail
    sub = lax.broadcasted_iota(jnp.int32, tail.shape, 1)
    return jnp.where(sub == 0, pltpu.roll(prev_tail, 1, axis=1), pltpu.roll(tail, 1, axis=1))


def _conv3(v, prev_ref, w_ref, c0, c1, time_strided):
    p = v.shape[0]
    tail = v[p - HIST_3:]
    hist = _history(prev_ref[:, :, c0:c1], tail, time_strided)
    prev_ref[:, :, c0:c1] = tail
    e = jnp.concatenate([hist, v], axis=0)
    out = e[0:p] * w_ref[0:1, c0:c1][None]
    for k in range(1, CONV_B):
        out = out + e[k:k + p] * w_ref[k:k + 1, c0:c1][None]
    return out


def _layer_kernel(x_ref, ha_ref, hb_ref, hf_ref,
                  n1g_ref, win_ref, bgate_ref, dwa_ref, bdwa_ref, lng_ref, lnb_ref, wa_ref,
                  dwb_ref, wb_ref, wo_ref, n2g_ref, wup_ref, dwf_ref, wdn_ref, fg_ref,
                  xo_ref, ta_ref, tb_ref, tf_ref,
                  h_ref, e_ref, c_ref, g_ref, pb_ref, pf_ref, *, t, time_strided, permute_in, permute_out,
                  final):
    ti = pl.program_id(1)
    p = t // SUBLANES
    n_lt = D_A // LANES

    @pl.when(ti == 0)
    def _():
        for j in range(n_lt):
            e_ref[j, p:p + HIST_A] = ha_ref[:, :, j * LANES:(j + 1) * LANES]
        pb_ref[...] = hb_ref[...]
        pf_ref[...] = hf_ref[...]

    x = x_ref[...]
    if permute_in:
        x = pltpu.einshape("sgd->gsd", x.reshape(SUBLANES, p, D_MODEL)).reshape(t, D_MODEL)
    h_ref[...] = _rms_norm(x, n1g_ref[...]).astype(_BF16)
    za = _dot(h_ref[...], win_ref[:, 0:2 * D_A])
    zb = _dot(h_ref[...], win_ref[:, 2 * D_A:GATE_COL0])

    u = (za[:, :D_A] * jax.nn.sigmoid(za[:, D_A:])).reshape(p, SUBLANES, D_A)
    tail = u[p - HIST_A:]
    ta_ref[...] = tail
    for j in range(n_lt):
        lanes = slice(j * LANES, (j + 1) * LANES)
        e_ref[j, 0:HIST_A] = _history(e_ref[j, p:p + HIST_A], tail[:, :, lanes], time_strided)
        e_ref[j, HIST_A:HIST_A + p] = u[:, :, lanes]

    cv = (zb[:, D_B:2 * D_B] * zb[:, 2 * D_B:]).reshape(p, SUBLANES, D_B)
    tb_ref[...] = cv[p - HIST_3:]
    v = _conv3(cv, pb_ref, dwb_ref, 0, D_B, time_strided).reshape(t, D_B)
    yb_in = (zb[:, :D_B] * v).astype(_BF16)

    tap0 = HIST_A - (CONV_A - 1)
    n_gb = p // CONV_GROUPS
    blocks_per_trip = n_lt * n_gb * GATES_PER_TRIP // GATE_CHUNKS

    def gate_chunk(c, after):
        lhs = h_ref[...]
        if after is not None:
            zero = (pltpu.bitcast(after[0], jnp.uint32) >> 16) >> 16
            top = pltpu.bitcast(pltpu.bitcast(lhs[0:16, 0:LANES], jnp.uint32) | zero, _BF16)
            lhs = jnp.concatenate(
                [jnp.concatenate([top, lhs[0:16, LANES:]], axis=1), lhs[16:]], axis=0)
        col = pl.multiple_of(c * GATE_W, GATE_W)
        g_ref[c] = _dot(lhs, win_ref[:, pl.ds(GATE_COL0 + col, GATE_W)]) + bgate_ref[:, pl.ds(col, GATE_W)]

    def conv_block(blk):
        j = blk // n_gb
        g0 = pl.multiple_of((blk % n_gb) * CONV_GROUPS, CONV_GROUPS)
        win = e_ref[j, pl.ds(g0, CONV_GROUPS + HIST_A)]
        acc = win[tap0:tap0 + CONV_GROUPS] * dwa_ref[j, 0:1, :][None]
        for k in range(1, CONV_A):
            acc = acc + win[tap0 + k:tap0 + k + CONV_GROUPS] * dwa_ref[j, k:k + 1, :][None]
        c_ref[j, pl.ds(g0, CONV_GROUPS)] = acc
        return acc

    blocks_per_gate = blocks_per_trip // GATES_PER_TRIP

    def conv_body(i, carry):
        after = None
        for q in range(GATES_PER_TRIP):
            gate_chunk(i * GATES_PER_TRIP + q, after)
            for r in range(blocks_per_gate):
                after = conv_block(i * blocks_per_trip + q * blocks_per_gate + r)
        return carry

    lax.fori_loop(0, GATE_CHUNKS // GATES_PER_TRIP, conv_body, 0)

    y_b = _dot(yb_in, wb_ref[...])
    conv = jnp.concatenate([c_ref[j] for j in range(n_lt)], axis=-1).reshape(t, D_A) + bdwa_ref[...]
    mu = jnp.mean(conv, axis=-1, keepdims=True)
    xc = conv - mu
    var = jnp.mean(xc * xc, axis=-1, keepdims=True)
    ya = xc * lax.rsqrt(var + LN_EPS) * lng_ref[...] + lnb_ref[...]
    y_a = _dot((ya * jax.nn.sigmoid(ya)).astype(_BF16), wa_ref[...])

    half = GATE_CHUNKS // 2
    g_a = jax.nn.sigmoid(jnp.concatenate([g_ref[c] for c in range(half)], axis=-1))
    g_b = jax.nn.sigmoid(jnp.concatenate([g_ref[c] for c in range(half, GATE_CHUNKS)], axis=-1))
    mix = (g_a * y_a + g_b * y_b).astype(_BF16)
    x1 = (x if permute_in else x_ref[...]) + _dot(mix, wo_ref[...])

    h2 = _rms_norm(x1, n2g_ref[...]).astype(_BF16)
    acc = x1
    def up(c0):
        w = jnp.concatenate([wup_ref[:, c0:c0 + FF_CHUNK], wup_ref[:, D_FF + c0:D_FF + c0 + FF_CHUNK]], axis=1)
        return _dot(h2, w)

    gv_next = up(0)
    for c0 in range(0, D_FF, FF_CHUNK):
        c1 = c0 + FF_CHUNK
        gv = gv_next
        if c1 < D_FF:
            gv_next = up(c1)
        gate = gv[:, :FF_CHUNK].reshape(p, SUBLANES, FF_CHUNK)
        tf_ref[:, :, c0:c1] = gate[p - HIST_3:]
        gc = _conv3(gate, pf_ref, dwf_ref, c0, c1, time_strided).reshape(t, FF_CHUNK)
        hh = (gc * jax.nn.sigmoid(gc) * gv[:, FF_CHUNK:]).astype(_BF16)
        acc = acc + _dot(hh, wdn_ref[c0:c1, :])
    if final:
        acc = _rms_norm(acc, fg_ref[...])
    if permute_out:
        acc = pltpu.einshape("gsd->sgd", acc.reshape(p, SUBLANES, D_MODEL)).reshape(t, D_MODEL)
    xo_ref[...] = acc


def _resident(shape, layer):
    zeros = (0,) * len(shape)
    return pl.BlockSpec((None,) + shape, lambda b, s: (layer,) + zeros, pipeline_mode=pl.Buffered(1))


def _state_spec(groups, ch):
    return pl.BlockSpec((None, groups, SUBLANES, ch), lambda b, i: (b, 0, 0, 0))


def _layer(x, ha, hb, hf, layer, p, t, time_strided, permute_in, permute_out, final):
    n, s, _ = x.shape
    x_spec = pl.BlockSpec((None, t, D_MODEL), lambda b, i: (b, i, 0))
    groups = t // SUBLANES
    n_lt = D_A // LANES
    return pl.pallas_call(
        functools.partial(_layer_kernel, t=t, time_strided=time_strided, permute_in=permute_in,
                          permute_out=permute_out, final=final),
        grid=(n, s // t),
        in_specs=[
            x_spec, _state_spec(HIST_A, D_A), _state_spec(HIST_3, D_B), _state_spec(HIST_3, D_FF),
            _resident((1, D_MODEL), layer), _resident((D_MODEL, D_IN), layer),
            _resident((1, 2 * D_MODEL), layer),
            _resident((n_lt, CONV_A, LANES), layer),
            _resident((1, D_A), layer), _resident((1, D_A), layer), _resident((1, D_A), layer),
            _resident((D_A, D_MODEL), layer), _resident((CONV_B, D_B), layer),
            _resident((D_B, D_MODEL), layer), _resident((D_MODEL, D_MODEL), layer),
            _resident((1, D_MODEL), layer), _resident((D_MODEL, 2 * D_FF), layer),
            _resident((CONV_F, D_FF), layer), _resident((D_FF, D_MODEL), layer),
            pl.BlockSpec((1, D_MODEL), lambda b, i: (0, 0)),
        ],
        out_specs=[x_spec, _state_spec(HIST_A, D_A), _state_spec(HIST_3, D_B), _state_spec(HIST_3, D_FF)],
        out_shape=[
            jax.ShapeDtypeStruct(x.shape, _F32),
            jax.ShapeDtypeStruct((n, HIST_A, SUBLANES, D_A), _F32),
            jax.ShapeDtypeStruct((n, HIST_3, SUBLANES, D_B), _F32),
            jax.ShapeDtypeStruct((n, HIST_3, SUBLANES, D_FF), _F32),
        ],
        scratch_shapes=[
            pltpu.VMEM((t, D_MODEL), _BF16),
            pltpu.VMEM((n_lt, HIST_A + groups, SUBLANES, LANES), _F32),
            pltpu.VMEM((n_lt, groups, SUBLANES, LANES), _F32),
            pltpu.VMEM((GATE_CHUNKS, t, GATE_W), _F32),
            pltpu.VMEM((HIST_3, SUBLANES, D_B), _F32),
            pltpu.VMEM((HIST_3, SUBLANES, D_FF), _F32),
        ],
        compiler_params=pltpu.CompilerParams(
            dimension_semantics=("arbitrary", "arbitrary"), vmem_limit_bytes=VMEM_LIMIT_BYTES),
        name="layer",
    )(x, ha, hb, hf, p["norm1_g"], p["w_in"], p["b_gate"], p["dw_a"], p["b_dw_a"],
      p["ln_a_g"], p["ln_a_b"], p["w_a_out"], p["dw_b"], p["w_b_out"], p["w_o"], p["norm2_g"], p["w_up"],
      p["dw_f"], p["w_down"], p["final_g"])


class _Layout:
    def __init__(self, n, s, time_strided):
        self.n, self.s, self.time_strided = n, s, time_strided
        if time_strided:
            self.t = min(s, BLOCK_ROWS)
            self.blocks = n
            assert s % self.t == 0
        else:
            self.t = min(SUBLANES * s, BLOCK_ROWS)
            self.blocks = n // SUBLANES
            assert n % SUBLANES == 0 and (SUBLANES * s) % self.t == 0
        self.p = self.t // SUBLANES
        assert self.p >= HIST_A and self.p % CONV_GROUPS == 0

    def to_rows(self, x):
        c = x.shape[-1]
        if self.time_strided:
            return x
        y = x.reshape(self.blocks, SUBLANES, self.s, c).transpose(0, 2, 1, 3)
        return y.reshape(self.blocks, SUBLANES * self.s, c)

    def from_rows(self, y):
        c = y.shape[-1]
        if self.time_strided:
            return y
        x = y.reshape(self.blocks, self.s, SUBLANES, c).transpose(0, 2, 1, 3)
        return x.reshape(self.n, self.s, c)

    def state_in(self, cache, groups):
        n, w, c = cache.shape
        if self.time_strided:
            st = jnp.zeros((n, groups, SUBLANES, c), cache.dtype)
            return st.at[:, groups - w:, SUBLANES - 1, :].set(cache)
        st = cache.reshape(self.blocks, SUBLANES, w, c).transpose(0, 2, 1, 3)
        return jnp.pad(st, ((0, 0), (groups - w, 0), (0, 0), (0, 0)))

    def state_out(self, tail, w):
        groups, c = tail.shape[1], tail.shape[-1]
        if self.time_strided:
            return tail[:, groups - w:, SUBLANES - 1, :]
        return tail[:, groups - w:].transpose(0, 2, 1, 3).reshape(self.n, w, c)


def kernel(x_prompt, x_sample, cache_conv_a, cache_conv_b, cache_ffn_conv, norm1_g, w_in, b_gate,
           dw_a, b_dw_a, ln_a_g, ln_a_b, w_a_out, dw_b, w_b_out, w_o, norm2_g, w_up, dw_f,
           w_down, final_g):
    depth = w_in.shape[0]
    n_p = x_prompt.shape[0]
    row = lambda a: a.reshape(a.shape[0], 1, a.shape[1])
    p = dict(
        norm1_g=row(norm1_g), w_in=w_in.astype(_BF16), b_gate=row(b_gate),
        dw_a=dw_a.reshape(depth, CONV_A, D_A // LANES, LANES).transpose(0, 2, 1, 3),
        b_dw_a=row(b_dw_a), ln_a_g=row(ln_a_g), ln_a_b=row(ln_a_b), w_a_out=w_a_out.astype(_BF16),
        dw_b=dw_b, w_b_out=w_b_out.astype(_BF16), w_o=w_o.astype(_BF16), norm2_g=row(norm2_g),
        w_up=w_up.astype(_BF16), dw_f=dw_f,
        w_down=w_down.astype(_BF16), final_g=final_g.reshape(1, D_MODEL),
    )
    groups = [
        (_Layout(n_p, x_prompt.shape[1], True), x_prompt,
         lambda l: (jnp.zeros((n_p, CONV_A - 1, D_A), _F32), jnp.zeros((n_p, CONV_B - 1, D_B), _F32),
                    jnp.zeros((n_p, CONV_F - 1, D_FF), _F32))),
        (_Layout(x_sample.shape[0], x_sample.shape[1], False), x_sample,
         lambda l: (cache_conv_a[l], cache_conv_b[l], cache_ffn_conv[l])),
    ]
    ys, new_a, new_b, new_f = [], [], [], []
    for lay, x, caches in groups:
        xr = lay.to_rows(x)
        sa, sb, sf = [], [], []
        for l in range(depth):
            ca, cb, cf = caches(l)
            xr, ta, tb, tf = _layer(xr, lay.state_in(ca, HIST_A), lay.state_in(cb, HIST_3),
                                    lay.state_in(cf, HIST_3), l, p, lay.t, lay.time_strided,
                                    lay.time_strided and l == 0, lay.time_strided and l == depth - 1,
                                    l == depth - 1)
            sa.append(lay.state_out(ta, CONV_A - 1))
            sb.append(lay.state_out(tb, CONV_B - 1))
            sf.append(lay.state_out(tf, CONV_F - 1))
        ys.append(lay.from_rows(xr))
        new_a.append(jnp.stack(sa)); new_b.append(jnp.stack(sb)); new_f.append(jnp.stack(sf))
    return (ys[0], ys[1], new_a[0], new_b[0], new_f[0], new_a[1], new_b[1], new_f[1])
```

```python
import functools

import jax
import jax.numpy as jnp
from jax import lax
from jax.experimental import pallas as pl
from jax.experimental.pallas import tpu as pltpu

D_MODEL = 1024
D_A = 512
D_B = 512
CONV_A = 31
CONV_B = 3
D_FF = 2816
CONV_F = 3
RMS_EPS = 1e-6
LN_EPS = 1e-5

SUBLANES = 8
LANES = 128
HIST_A = 32
HIST_3 = CONV_B - 1
CONV_GROUPS = 8
FF_CHUNK = 256
CONV_TRIPS = 4
GATE_COL0 = 2 * D_A + 3 * D_B
D_IN = GATE_COL0 + 2 * D_MODEL
BLOCK_ROWS = 256
VMEM_LIMIT_BYTES = 56 * 1024 * 1024

_F32 = jnp.float32
_BF16 = jnp.bfloat16


def _rms_norm(x, g):
    y = x * lax.rsqrt(jnp.mean(x * x, axis=-1, keepdims=True) + RMS_EPS)
    return y * g


def _dot(a, b):
    return jnp.dot(a, b, preferred_element_type=_F32)


def _history(prev_tail, tail, time_strided):
    if not time_strided:
        return prev_tail
    sub = lax.broadcasted_iota(jnp.int32, tail.shape, 1)
    return jnp.where(sub == 0, pltpu.roll(prev_tail, 1, axis=1), pltpu.roll(tail, 1, axis=1))


def _conv3(v, prev_ref, w_ref, c0, c1, time_strided):
    p = v.shape[0]
    tail = v[p - HIST_3:]
    hist = _history(prev_ref[:, :, c0:c1], tail, time_strided)
    prev_ref[:, :, c0:c1] = tail
    e = jnp.concatenate([hist, v], axis=0)
    out = e[0:p] * w_ref[0:1, c0:c1][None]
    for k in range(1, CONV_B):
        out = out + e[k:k + p] * w_ref[k:k + 1, c0:c1][None]
    return out


def _layer_kernel(x_ref, ha_ref, hb_ref, hf_ref,
                  n1g_ref, win_ref, bgate_ref, dwa_ref, bdwa_ref, lng_ref, lnb_ref, wa_ref,
                  dwb_ref, wb_ref, wo_ref, n2g_ref, wup_ref, dwf_ref, wdn_ref, fg_ref,
                  xo_ref, ta_ref, tb_ref, tf_ref,
                  e_ref, c_ref, pb_ref, pf_ref, *, t, time_strided, permute_in, permute_out,
                  final):
    ti = pl.program_id(1)
    p = t // SUBLANES
    n_lt = D_A // LANES

    @pl.when(ti == 0)
    def _():
        for j in range(n_lt):
            e_ref[j, p:p + HIST_A] = ha_ref[:, :, j * LANES:(j + 1) * LANES]
        pb_ref[...] = hb_ref[...]
        pf_ref[...] = hf_ref[...]

    x = x_ref[...]
    if permute_in:
        x = pltpu.einshape("sgd->gsd", x.reshape(SUBLANES, p, D_MODEL)).reshape(t, D_MODEL)
    h = _rms_norm(x, n1g_ref[...]).astype(_BF16)
    za = _dot(h, win_ref[:, 0:2 * D_A])
    zb = _dot(h, win_ref[:, 2 * D_A:GATE_COL0])
    gates = _dot(h, win_ref[:, GATE_COL0:]) + bgate_ref[...]

    u = (za[:, :D_A] * jax.nn.sigmoid(za[:, D_A:])).reshape(p, SUBLANES, D_A)
    tail = u[p - HIST_A:]
    ta_ref[...] = tail
    for j in range(n_lt):
        lanes = slice(j * LANES, (j + 1) * LANES)
        e_ref[j, 0:HIST_A] = _history(e_ref[j, p:p + HIST_A], tail[:, :, lanes], time_strided)
        e_ref[j, HIST_A:HIST_A + p] = u[:, :, lanes]

    cv = (zb[:, D_B:2 * D_B] * zb[:, 2 * D_B:]).reshape(p, SUBLANES, D_B)
    tb_ref[...] = cv[p - HIST_3:]
    v = _conv3(cv, pb_ref, dwb_ref, 0, D_B, time_strided).reshape(t, D_B)
    yb_in = (zb[:, :D_B] * v).astype(_BF16)

    tap0 = HIST_A - (CONV_A - 1)
    n_gb = p // CONV_GROUPS
    blocks_per_trip = n_lt * n_gb // CONV_TRIPS

    def conv_body(i, carry):
        for q in range(blocks_per_trip):
            blk = i * blocks_per_trip + q
            j = blk // n_gb
            g0 = pl.multiple_of((blk % n_gb) * CONV_GROUPS, CONV_GROUPS)
            win = e_ref[j, pl.ds(g0, CONV_GROUPS + HIST_A)]
            acc = win[tap0:tap0 + CONV_GROUPS] * dwa_ref[j, 0:1, :][None]
            for k in range(1, CONV_A):
                acc = acc + win[tap0 + k:tap0 + k + CONV_GROUPS] * dwa_ref[j, k:k + 1, :][None]
            c_ref[j, pl.ds(g0, CONV_GROUPS)] = acc
        return carry

    lax.fori_loop(0, CONV_TRIPS, conv_body, 0)

    y_b = _dot(yb_in, wb_ref[...])
    conv = jnp.concatenate([c_ref[j] for j in range(n_lt)], axis=-1).reshape(t, D_A) + bdwa_ref[...]
    mu = jnp.mean(conv, axis=-1, keepdims=True)
    xc = conv - mu
    var = jnp.mean(xc * xc, axis=-1, keepdims=True)
    ya = xc * lax.rsqrt(var + LN_EPS) * lng_ref[...] + lnb_ref[...]
    y_a = _dot((ya * jax.nn.sigmoid(ya)).astype(_BF16), wa_ref[...])

    g_a = jax.nn.sigmoid(gates[:, :D_MODEL])
    g_b = jax.nn.sigmoid(gates[:, D_MODEL:])
    mix = (g_a * y_a + g_b * y_b).astype(_BF16)
    x1 = x + _dot(mix, wo_ref[...])

    h2 = _rms_norm(x1, n2g_ref[...]).astype(_BF16)
    acc = x1
    def up(c0):
        w = jnp.concatenate([wup_ref[:, c0:c0 + FF_CHUNK], wup_ref[:, D_FF + c0:D_FF + c0 + FF_CHUNK]], axis=1)
        return _dot(h2, w)

    gv_next = up(0)
    for c0 in range(0, D_FF, FF_CHUNK):
        c1 = c0 + FF_CHUNK
        gv = gv_next
        if c1 < D_FF:
            gv_next = up(c1)
        gate = gv[:, :FF_CHUNK].reshape(p, SUBLANES, FF_CHUNK)
        tf_ref[:, :, c0:c1] = gate[p - HIST_3:]
        gc = _conv3(gate, pf_ref, dwf_ref, c0, c1, time_strided).reshape(t, FF_CHUNK)
        hh = (gc * jax.nn.sigmoid(gc) * gv[:, FF_CHUNK:]).astype(_BF16)
        acc = acc + _dot(hh, wdn_ref[c0:c1, :])
    if final:
        acc = _rms_norm(acc, fg_ref[...])
    if permute_out:
        acc = pltpu.einshape("gsd->sgd", acc.reshape(p, SUBLANES, D_MODEL)).reshape(t, D_MODEL)
    xo_ref[...] = acc


def _resident(shape, layer):
    zeros = (0,) * len(shape)
    return pl.BlockSpec((None,) + shape, lambda b, s: (layer,) + zeros, pipeline_mode=pl.Buffered(1))


def _state_spec(groups, ch):
    return pl.BlockSpec((None, groups, SUBLANES, ch), lambda b, i: (b, 0, 0, 0))


def _layer(x, ha, hb, hf, layer, p, t, time_strided, permute_in, permute_out, final):
    n, s, _ = x.shape
    x_spec = pl.BlockSpec((None, t, D_MODEL), lambda b, i: (b, i, 0))
    groups = t // SUBLANES
    n_lt = D_A // LANES
    return pl.pallas_call(
        functools.partial(_layer_kernel, t=t, time_strided=time_strided, permute_in=permute_in,
                          permute_out=permute_out, final=final),
        grid=(n, s // t),
        in_specs=[
            x_spec, _state_spec(HIST_A, D_A), _state_spec(HIST_3, D_B), _state_spec(HIST_3, D_FF),
            _resident((1, D_MODEL), layer), _resident((D_MODEL, D_IN), layer),
            _resident((1, 2 * D_MODEL), layer),
            _resident((n_lt, CONV_A, LANES), layer),
            _resident((1, D_A), layer), _resident((1, D_A), layer), _resident((1, D_A), layer),
            _resident((D_A, D_MODEL), layer), _resident((CONV_B, D_B), layer),
            _resident((D_B, D_MODEL), layer), _resident((D_MODEL, D_MODEL), layer),
            _resident((1, D_MODEL), layer), _resident((D_MODEL, 2 * D_FF), layer),
            _resident((CONV_F, D_FF), layer), _resident((D_FF, D_MODEL), layer),
            pl.BlockSpec((1, D_MODEL), lambda b, i: (0, 0)),
        ],
        out_specs=[x_spec, _state_spec(HIST_A, D_A), _state_spec(HIST_3, D_B), _state_spec(HIST_3, D_FF)],
        out_shape=[
            jax.ShapeDtypeStruct(x.shape, _F32),
            jax.ShapeDtypeStruct((n, HIST_A, SUBLANES, D_A), _F32),
            jax.ShapeDtypeStruct((n, HIST_3, SUBLANES, D_B), _F32),
            jax.ShapeDtypeStruct((n, HIST_3, SUBLANES, D_FF), _F32),
        ],
        scratch_shapes=[
            pltpu.VMEM((n_lt, HIST_A + groups, SUBLANES, LANES), _F32),
            pltpu.VMEM((n_lt, groups, SUBLANES, LANES), _F32),
            pltpu.VMEM((HIST_3, SUBLANES, D_B), _F32),
            pltpu.VMEM((HIST_3, SUBLANES, D_FF), _F32),
        ],
        compiler_params=pltpu.CompilerParams(
            dimension_semantics=("arbitrary", "arbitrary"), vmem_limit_bytes=VMEM_LIMIT_BYTES),
        name="layer",
    )(x, ha, hb, hf, p["norm1_g"], p["w_in"], p["b_gate"], p["dw_a"], p["b_dw_a"],
      p["ln_a_g"], p["ln_a_b"], p["w_a_out"], p["dw_b"], p["w_b_out"], p["w_o"], p["norm2_g"], p["w_up"],
      p["dw_f"], p["w_down"], p["final_g"])


class _Layout:
    def __init__(self, n, s, time_strided):
        self.n, self.s, self.time_strided = n, s, time_strided
        if time_strided:
            self.t = min(s, BLOCK_ROWS)
            self.blocks = n
            assert s % self.t == 0
        else:
            self.t = min(SUBLANES * s, BLOCK_ROWS)
            self.blocks = n // SUBLANES
            assert n % SUBLANES == 0 and (SUBLANES * s) % self.t == 0
        self.p = self.t // SUBLANES
        assert self.p >= HIST_A and self.p % CONV_GROUPS == 0

    def to_rows(self, x):
        c = x.shape[-1]
        if self.time_strided:
            return x
        y = x.reshape(self.blocks, SUBLANES, self.s, c).transpose(0, 2, 1, 3)
        return y.reshape(self.blocks, SUBLANES * self.s, c)

    def from_rows(self, y):
        c = y.shape[-1]
        if self.time_strided:
            return y
        x = y.reshape(self.blocks, self.s, SUBLANES, c).transpose(0, 2, 1, 3)
        return x.reshape(self.n, self.s, c)

    def state_in(self, cache, groups):
        n, w, c = cache.shape
        if self.time_strided:
            st = jnp.zeros((n, groups, SUBLANES, c), cache.dtype)
            return st.at[:, groups - w:, SUBLANES - 1, :].set(cache)
        st = cache.reshape(self.blocks, SUBLANES, w, c).transpose(0, 2, 1, 3)
        return jnp.pad(st, ((0, 0), (groups - w, 0), (0, 0), (0, 0)))

    def state_out(self, tail, w):
        groups, c = tail.shape[1], tail.shape[-1]
        if self.time_strided:
            return tail[:, groups - w:, SUBLANES - 1, :]
        return tail[:, groups - w:].transpose(0, 2, 1, 3).reshape(self.n, w, c)


def kernel(x_prompt, x_sample, cache_conv_a, cache_conv_b, cache_ffn_conv, norm1_g, w_in, b_gate,
           dw_a, b_dw_a, ln_a_g, ln_a_b, w_a_out, dw_b, w_b_out, w_o, norm2_g, w_up, dw_f,
           w_down, final_g):
    depth = w_in.shape[0]
    n_p = x_prompt.shape[0]
    row = lambda a: a.reshape(a.shape[0], 1, a.shape[1])
    p = dict(
        norm1_g=row(norm1_g), w_in=w_in.astype(_BF16), b_gate=row(b_gate),
        dw_a=dw_a.reshape(depth, CONV_A, D_A // LANES, LANES).transpose(0, 2, 1, 3),
        b_dw_a=row(b_dw_a), ln_a_g=row(ln_a_g), ln_a_b=row(ln_a_b), w_a_out=w_a_out.astype(_BF16),
        dw_b=dw_b, w_b_out=w_b_out.astype(_BF16), w_o=w_o.astype(_BF16), norm2_g=row(norm2_g),
        w_up=w_up.astype(_BF16), dw_f=dw_f,
        w_down=w_down.astype(_BF16), final_g=final_g.reshape(1, D_MODEL),
    )
    groups = [
        (_Layout(n_p, x_prompt.shape[1], True), x_prompt,
         lambda l: (jnp.zeros((n_p, CONV_A - 1, D_A), _F32), jnp.zeros((n_p, CONV_B - 1, D_B), _F32),
                    jnp.zeros((n_p, CONV_F - 1, D_FF), _F32))),
        (_Layout(x_sample.shape[0], x_sample.shape[1], False), x_sample,
         lambda l: (cache_conv_a[l], cache_conv_b[l], cache_ffn_conv[l])),
    ]
    ys, new_a, new_b, new_f = [], [], [], []
    for lay, x, caches in groups:
        xr = lay.to_rows(x)
        sa, sb, sf = [], [], []
        for l in range(depth):
            ca, cb, cf = caches(l)
            xr, ta, tb, tf = _layer(xr, lay.state_in(ca, HIST_A), lay.state_in(cb, HIST_3),
                                    lay.state_in(cf, HIST_3), l, p, lay.t, lay.time_strided,
                                    lay.time_strided and l == 0, lay.time_strided and l == depth - 1,
                                    l == depth - 1)
            sa.append(lay.state_out(ta, CONV_A - 1))
            sb.append(lay.state_out(tb, CONV_B - 1))
            sf.append(lay.state_out(tf, CONV_F - 1))
        ys.append(lay.from_rows(xr))
        new_a.append(jnp.stack(sa)); new_b.append(jnp.stack(sb)); new_f.append(jnp.stack(sf))
    return (ys[0], ys[1], new_a[0], new_b[0], new_f[0], new_a[1], new_b[1], new_f[1])
```

```python
import functools

import jax
import jax.numpy as jnp
from jax import lax
from jax.experimental import pallas as pl
from jax.experimental.pallas import tpu as pltpu

D_MODEL = 1024
D_A = 512
D_B = 512
CONV_A = 31
CONV_B = 3
D_FF = 2816
CONV_F = 3
RMS_EPS = 1e-6
LN_EPS = 1e-5

SUBLANES = 8
LANES = 128
HIST_A = 32
HIST_3 = CONV_B - 1
CONV_GROUPS = 8
FF_CHUNK = 256
CONV_TRIPS = 4
GATE_COL0 = 2 * D_A + 3 * D_B
D_IN = GATE_COL0 + 2 * D_MODEL
BLOCK_ROWS = 256
VMEM_LIMIT_BYTES = 56 * 1024 * 1024

_F32 = jnp.float32
_BF16 = jnp.bfloat16


def _rms_norm(x, g):
    y = x * lax.rsqrt(jnp.mean(x * x, axis=-1, keepdims=True) + RMS_EPS)
    return y * g


def _dot(a, b):
    return jnp.dot(a, b, preferred_element_type=_F32)


def _history(prev_tail, tail, time_strided):
    if not time_strided:
        return prev_tail
    sub = lax.broadcasted_iota(jnp.int32, tail.shape, 1)
    return jnp.where(sub == 0, pltpu.roll(prev_tail, 1, axis=1), pltpu.roll(tail, 1, axis=1))


def _conv3(v, prev_ref, w_ref, c0, c1, time_strided):
    p = v.shape[0]
    tail = v[p - HIST_3:]
    hist = _history(prev_ref[:, :, c0:c1], tail, time_strided)
    prev_ref[:, :, c0:c1] = tail
    e = jnp.concatenate([hist, v], axis=0)
    out = e[0:p] * w_ref[0:1, c0:c1][None]
    for k in range(1, CONV_B):
        out = out + e[k:k + p] * w_ref[k:k + 1, c0:c1][None]
    return out


def _layer_kernel(x_ref, ha_ref, hb_ref, hf_ref,
                  n1g_ref, win_ref, bgate_ref, dwa_ref, bdwa_ref, lng_ref, lnb_ref, wa_ref,
                  dwb_ref, wb_ref, wo_ref, n2g_ref, wup_ref, dwf_ref, wdn_ref, fg_ref,
                  xo_ref, ta_ref, tb_ref, tf_ref,
                  e_ref, c_ref, pb_ref, pf_ref, *, t, time_strided, permute_in, permute_out,
                  final):
    ti = pl.program_id(1)
    p = t // SUBLANES
    n_lt = D_A // LANES

    @pl.when(ti == 0)
    def _():
        for j in range(n_lt):
            e_ref[j, p:p + HIST_A] = ha_ref[:, :, j * LANES:(j + 1) * LANES]
        pb_ref[...] = hb_ref[...]
        pf_ref[...] = hf_ref[...]

    x = x_ref[...]
    if permute_in:
        x = pltpu.einshape("sgd->gsd", x.reshape(SUBLANES, p, D_MODEL)).reshape(t, D_MODEL)
    h = _rms_norm(x, n1g_ref[...]).astype(_BF16)
    za = _dot(h, win_ref[:, 0:2 * D_A])
    gates = jax.nn.sigmoid(_dot(h, win_ref[:, GATE_COL0:]) + bgate_ref[...])
    zb = _dot(h, win_ref[:, 2 * D_A:GATE_COL0])

    u = (za[:, :D_A] * jax.nn.sigmoid(za[:, D_A:])).reshape(p, SUBLANES, D_A)
    tail = u[p - HIST_A:]
    ta_ref[...] = tail
    for j in range(n_lt):
        lanes = slice(j * LANES, (j + 1) * LANES)
        e_ref[j, 0:HIST_A] = _history(e_ref[j, p:p + HIST_A], tail[:, :, lanes], time_strided)
        e_ref[j, HIST_A:HIST_A + p] = u[:, :, lanes]

    cv = (zb[:, D_B:2 * D_B] * zb[:, 2 * D_B:]).reshape(p, SUBLANES, D_B)
    tb_ref[...] = cv[p - HIST_3:]
    v = _conv3(cv, pb_ref, dwb_ref, 0, D_B, time_strided).reshape(t, D_B)
    yb_in = (zb[:, :D_B] * v).astype(_BF16)

    tap0 = HIST_A - (CONV_A - 1)
    n_gb = p // CONV_GROUPS
    blocks_per_trip = n_lt * n_gb // CONV_TRIPS

    def conv_body(i, carry):
        for q in range(blocks_per_trip):
            blk = i * blocks_per_trip + q
            j = blk // n_gb
            g0 = pl.multiple_of((blk % n_gb) * CONV_GROUPS, CONV_GROUPS)
            win = e_ref[j, pl.ds(g0, CONV_GROUPS + HIST_A)]
            acc = win[tap0:tap0 + CONV_GROUPS] * dwa_ref[j, 0:1, :][None]
            for k in range(1, CONV_A):
                acc = acc + win[tap0 + k:tap0 + k + CONV_GROUPS] * dwa_ref[j, k:k + 1, :][None]
            c_ref[j, pl.ds(g0, CONV_GROUPS)] = acc
        return carry

    lax.fori_loop(0, CONV_TRIPS, conv_body, 0)

    y_b = _dot(yb_in, wb_ref[...])
    conv = jnp.concatenate([c_ref[j] for j in range(n_lt)], axis=-1).reshape(t, D_A) + bdwa_ref[...]
    mu = jnp.mean(conv, axis=-1, keepdims=True)
    xc = conv - mu
    var = jnp.mean(xc * xc, axis=-1, keepdims=True)
    ya = xc * lax.rsqrt(var + LN_EPS) * lng_ref[...] + lnb_ref[...]
    y_a = _dot((ya * jax.nn.sigmoid(ya)).astype(_BF16), wa_ref[...])

    mix = (gates[:, :D_MODEL] * y_a + gates[:, D_MODEL:] * y_b).astype(_BF16)
    x1 = x + _dot(mix, wo_ref[...])

    h2 = _rms_norm(x1, n2g_ref[...]).astype(_BF16)
    acc = x1
    def up(c0):
        w = jnp.concatenate([wup_ref[:, c0:c0 + FF_CHUNK], wup_ref[:, D_FF + c0:D_FF + c0 + FF_CHUNK]], axis=1)
        return _dot(h2, w)

    gv_next = up(0)
    for c0 in range(0, D_FF, FF_CHUNK):
        c1 = c0 + FF_CHUNK
        gv = gv_next
        if c1 < D_FF:
            gv_next = up(c1)
        gate = gv[:, :FF_CHUNK].reshape(p, SUBLANES, FF_CHUNK)
        tf_ref[:, :, c0:c1] = gate[p - HIST_3:]
        gc = _conv3(gate, pf_ref, dwf_ref, c0, c1, time_strided).reshape(t, FF_CHUNK)
        hh = (gc * jax.nn.sigmoid(gc) * gv[:, FF_CHUNK:]).astype(_BF16)
        acc = acc + _dot(hh, wdn_ref[c0:c1, :])
    if final:
        acc = _rms_norm(acc, fg_ref[...])
    if permute_out:
        acc = pltpu.einshape("gsd->sgd", acc.reshape(p, SUBLANES, D_MODEL)).reshape(t, D_MODEL)
    xo_ref[...] = acc


def _resident(shape, layer):
    zeros = (0,) * len(shape)
    return pl.BlockSpec((None,) + shape, lambda b, s: (layer,) + zeros, pipeline_mode=pl.Buffered(1))


def _state_spec(groups, ch):
    return pl.BlockSpec((None, groups, SUBLANES, ch), lambda b, i: (b, 0, 0, 0))


def _layer(x, ha, hb, hf, layer, p, t, time_strided, permute_in, permute_out, final):
    n, s, _ = x.shape
    x_spec = pl.BlockSpec((None, t, D_MODEL), lambda b, i: (b, i, 0))
    groups = t // SUBLANES
    n_lt = D_A // LANES
    return pl.pallas_call(
        functools.partial(_layer_kernel, t=t, time_strided=time_strided, permute_in=permute_in,
                          permute_out=permute_out, final=final),
        grid=(n, s // t),
        in_specs=[
            x_spec, _state_spec(HIST_A, D_A), _state_spec(HIST_3, D_B), _state_spec(HIST_3, D_FF),
            _resident((1, D_MODEL), layer), _resident((D_MODEL, D_IN), layer),
            _resident((1, 2 * D_MODEL), layer),
            _resident((n_lt, CONV_A, LANES), layer),
            _resident((1, D_A), layer), _resident((1, D_A), layer), _resident((1, D_A), layer),
            _resident((D_A, D_MODEL), layer), _resident((CONV_B, D_B), layer),
            _resident((D_B, D_MODEL), layer), _resident((D_MODEL, D_MODEL), layer),
            _resident((1, D_MODEL), layer), _resident((D_MODEL, 2 * D_FF), layer),
            _resident((CONV_F, D_FF), layer), _resident((D_FF, D_MODEL), layer),
            pl.BlockSpec((1, D_MODEL), lambda b, i: (0, 0)),
        ],
        out_specs=[x_spec, _state_spec(HIST_A, D_A), _state_spec(HIST_3, D_B), _state_spec(HIST_3, D_FF)],
        out_shape=[
            jax.ShapeDtypeStruct(x.shape, _F32),
            jax.ShapeDtypeStruct((n, HIST_A, SUBLANES, D_A), _F32),
            jax.ShapeDtypeStruct((n, HIST_3, SUBLANES, D_B), _F32),
            jax.ShapeDtypeStruct((n, HIST_3, SUBLANES, D_FF), _F32),
        ],
        scratch_shapes=[
            pltpu.VMEM((n_lt, HIST_A + groups, SUBLANES, LANES), _F32),
            pltpu.VMEM((n_lt, groups, SUBLANES, LANES), _F32),
            pltpu.VMEM((HIST_3, SUBLANES, D_B), _F32),
            pltpu.VMEM((HIST_3, SUBLANES, D_FF), _F32),
        ],
        compiler_params=pltpu.CompilerParams(
            dimension_semantics=("arbitrary", "arbitrary"), vmem_limit_bytes=VMEM_LIMIT_BYTES),
        name="layer",
    )(x, ha, hb, hf, p["norm1_g"], p["w_in"], p["b_gate"], p["dw_a"], p["b_dw_a"],
      p["ln_a_g"], p["ln_a_b"], p["w_a_out"], p["dw_b"], p["w_b_out"], p["w_o"], p["norm2_g"], p["w_up"],
      p["dw_f"], p["w_down"], p["final_g"])


class _Layout:
    def __init__(self, n, s, time_strided):
        self.n, self.s, self.time_strided = n, s, time_strided
        if time_strided:
            self.t = min(s, BLOCK_ROWS)
            self.blocks = n
            assert s % self.t == 0
        else:
            self.t = min(SUBLANES * s, BLOCK_ROWS)
            self.blocks = n // SUBLANES
            assert n % SUBLANES == 0 and (SUBLANES * s) % self.t == 0
        self.p = self.t // SUBLANES
        assert self.p >= HIST_A and self.p % CONV_GROUPS == 0

    def to_rows(self, x):
        c = x.shape[-1]
        if self.time_strided:
            return x
        y = x.reshape(self.blocks, SUBLANES, self.s, c).transpose(0, 2, 1, 3)
        return y.reshape(self.blocks, SUBLANES * self.s, c)

    def from_rows(self, y):
        c = y.shape[-1]
        if self.time_strided:
            return y
        x = y.reshape(self.blocks, self.s, SUBLANES, c).transpose(0, 2, 1, 3)
        return x.reshape(self.n, self.s, c)

    def state_in(self, cache, groups):
        n, w, c = cache.shape
        if self.time_strided:
            st = jnp.zeros((n, groups, SUBLANES, c), cache.dtype)
            return st.at[:, groups - w:, SUBLANES - 1, :].set(cache)
        st = cache.reshape(self.blocks, SUBLANES, w, c).transpose(0, 2, 1, 3)
        return jnp.pad(st, ((0, 0), (groups - w, 0), (0, 0), (0, 0)))

    def state_out(self, tail, w):
        groups, c = tail.shape[1], tail.shape[-1]
        if self.time_strided:
            return tail[:, groups - w:, SUBLANES - 1, :]
        return tail[:, groups - w:].transpose(0, 2, 1, 3).reshape(self.n, w, c)


def kernel(x_prompt, x_sample, cache_conv_a, cache_conv_b, cache_ffn_conv, norm1_g, w_in, b_gate,
           dw_a, b_dw_a, ln_a_g, ln_a_b, w_a_out, dw_b, w_b_out, w_o, norm2_g, w_up, dw_f,
           w_down, final_g):
    depth = w_in.shape[0]
    n_p = x_prompt.shape[0]
    row = lambda a: a.reshape(a.shape[0], 1, a.shape[1])
    p = dict(
        norm1_g=row(norm1_g), w_in=w_in.astype(_BF16), b_gate=row(b_gate),
        dw_a=dw_a.reshape(depth, CONV_A, D_A // LANES, LANES).transpose(0, 2, 1, 3),
        b_dw_a=row(b_dw_a), ln_a_g=row(ln_a_g), ln_a_b=row(ln_a_b), w_a_out=w_a_out.astype(_BF16),
        dw_b=dw_b, w_b_out=w_b_out.astype(_BF16), w_o=w_o.astype(_BF16), norm2_g=row(norm2_g),
        w_up=w_up.astype(_BF16), dw_f=dw_f,
        w_down=w_down.astype(_BF16), final_g=final_g.reshape(1, D_MODEL),
    )
    groups = [
        (_Layout(n_p, x_prompt.shape[1], True), x_prompt,
         lambda l: (jnp.zeros((n_p, CONV_A - 1, D_A), _F32), jnp.zeros((n_p, CONV_B - 1, D_B), _F32),
                    jnp.zeros((n_p, CONV_F - 1, D_FF), _F32))),
        (_Layout(x_sample.shape[0], x_sample.shape[1], False), x_sample,
         lambda l: (cache_conv_a[l], cache_conv_b[l], cache_ffn_conv[l])),
    ]
    ys, new_a, new_b, new_f = [], [], [], []
    for lay, x, caches in groups:
        xr = lay.to_rows(x)
        sa, sb, sf = [], [], []
        for l in range(depth):
            ca, cb, cf = caches(l)
            xr, ta, tb, tf = _layer(xr, lay.state_in(ca, HIST_A), lay.state_in(cb, HIST_3),
                                    lay.state_in(cf, HIST_3), l, p, lay.t, lay.time_strided,
                                    lay.time_strided and l == 0, lay.time_strided and l == depth - 1,
                                    l == depth - 1)
            sa.append(lay.state_out(ta, CONV_A - 1))
            sb.append(lay.state_out(tb, CONV_B - 1))
            sf.append(lay.state_out(tf, CONV_F - 1))
        ys.append(lay.from_rows(xr))
        new_a.append(jnp.stack(sa)); new_b.append(jnp.stack(sb)); new_f.append(jnp.stack(sf))
    return (ys[0], ys[1], new_a[0], new_b[0], new_f[0], new_a[1], new_b[1], new_f[1])
```

```python
import functools

import jax
import jax.numpy as jnp
from jax import lax
from jax.experimental import pallas as pl
from jax.experimental.pallas import tpu as pltpu

D_MODEL = 1024
D_A = 512
D_B = 512
CONV_A = 31
CONV_B = 3
D_FF = 2816
CONV_F = 3
RMS_EPS = 1e-6
LN_EPS = 1e-5

SUBLANES = 8
LANES = 128
HIST_A = 32
HIST_3 = CONV_B - 1
CONV_GROUPS = 8
FF_CHUNK = 256
CONV_TRIPS = 8
GATE_COL0 = 2 * D_A + 3 * D_B
D_IN = GATE_COL0 + 2 * D_MODEL
BLOCK_ROWS = 512
VMEM_LIMIT_BYTES = 60 * 1024 * 1024

_F32 = jnp.float32
_BF16 = jnp.bfloat16


def _rms_norm(x, g):
    y = x * lax.rsqrt(jnp.mean(x * x, axis=-1, keepdims=True) + RMS_EPS)
    return y * g


def _dot(a, b):
    return jnp.dot(a, b, preferred_element_type=_F32)


def _history(prev_tail, tail, time_strided):
    if not time_strided:
        return prev_tail
    sub = lax.broadcasted_iota(jnp.int32, tail.shape, 1)
    return jnp.where(sub == 0, pltpu.roll(prev_tail, 1, axis=1), pltpu.roll(tail, 1, axis=1))


def _conv3(v, prev_ref, w_ref, c0, c1, time_strided):
    p = v.shape[0]
    tail = v[p - HIST_3:]
    hist = _history(prev_ref[:, :, c0:c1], tail, time_strided)
    prev_ref[:, :, c0:c1] = tail
    e = jnp.concatenate([hist, v], axis=0)
    out = e[0:p] * w_ref[0:1, c0:c1][None]
    for k in range(1, CONV_B):
        out = out + e[k:k + p] * w_ref[k:k + 1, c0:c1][None]
    return out


def _layer_kernel(x_ref, ha_ref, hb_ref, hf_ref,
                  n1g_ref, win_ref, bgate_ref, dwa_ref, bdwa_ref, lng_ref, lnb_ref, wa_ref,
                  dwb_ref, wb_ref, wo_ref, n2g_ref, wup_ref, dwf_ref, wdn_ref, fg_ref,
                  xo_ref, ta_ref, tb_ref, tf_ref,
                  e_ref, c_ref, pb_ref, pf_ref, *, t, time_strided, permute_in, permute_out,
                  final):
    ti = pl.program_id(1)
    p = t // SUBLANES
    n_lt = D_A // LANES

    @pl.when(ti == 0)
    def _():
        for j in range(n_lt):
            e_ref[j, p:p + HIST_A] = ha_ref[:, :, j * LANES:(j + 1) * LANES]
        pb_ref[...] = hb_ref[...]
        pf_ref[...] = hf_ref[...]

    x = x_ref[...]
    if permute_in:
        x = pltpu.einshape("sgd->gsd", x.reshape(SUBLANES, p, D_MODEL)).reshape(t, D_MODEL)
    h = _rms_norm(x, n1g_ref[...]).astype(_BF16)
    za = _dot(h, win_ref[:, 0:2 * D_A])
    gates = jax.nn.sigmoid(_dot(h, win_ref[:, GATE_COL0:]) + bgate_ref[...])
    zb = _dot(h, win_ref[:, 2 * D_A:GATE_COL0])

    u = (za[:, :D_A] * jax.nn.sigmoid(za[:, D_A:])).reshape(p, SUBLANES, D_A)
    tail = u[p - HIST_A:]
    ta_ref[...] = tail
    for j in range(n_lt):
        lanes = slice(j * LANES, (j + 1) * LANES)
        e_ref[j, 0:HIST_A] = _history(e_ref[j, p:p + HIST_A], tail[:, :, lanes], time_strided)
        e_ref[j, HIST_A:HIST_A + p] = u[:, :, lanes]

    cv = (zb[:, D_B:2 * D_B] * zb[:, 2 * D_B:]).reshape(p, SUBLANES, D_B)
    tb_ref[...] = cv[p - HIST_3:]
    v = _conv3(cv, pb_ref, dwb_ref, 0, D_B, time_strided).reshape(t, D_B)
    yb_in = (zb[:, :D_B] * v).astype(_BF16)

    tap0 = HIST_A - (CONV_A - 1)
    n_gb = p // CONV_GROUPS
    blocks_per_trip = n_lt * n_gb // CONV_TRIPS

    def conv_body(i, carry):
        for q in range(blocks_per_trip):
            blk = i * blocks_per_trip + q
            j = blk // n_gb
            g0 = pl.multiple_of((blk % n_gb) * CONV_GROUPS, CONV_GROUPS)
            win = e_ref[j, pl.ds(g0, CONV_GROUPS + HIST_A)]
            acc = win[tap0:tap0 + CONV_GROUPS] * dwa_ref[j, 0:1, :][None]
            for k in range(1, CONV_A):
                acc = acc + win[tap0 + k:tap0 + k + CONV_GROUPS] * dwa_ref[j, k:k + 1, :][None]
            c_ref[j, pl.ds(g0, CONV_GROUPS)] = acc
        return carry

    lax.fori_loop(0, CONV_TRIPS, conv_body, 0)

    y_b = _dot(yb_in, wb_ref[...])
    conv = jnp.concatenate([c_ref[j] for j in range(n_lt)], axis=-1).reshape(t, D_A) + bdwa_ref[...]
    mu = jnp.mean(conv, axis=-1, keepdims=True)
    xc = conv - mu
    var = jnp.mean(xc * xc, axis=-1, keepdims=True)
    ya = xc * lax.rsqrt(var + LN_EPS) * lng_ref[...] + lnb_ref[...]
    y_a = _dot((ya * jax.nn.sigmoid(ya)).astype(_BF16), wa_ref[...])

    mix = (gates[:, :D_MODEL] * y_a + gates[:, D_MODEL:] * y_b).astype(_BF16)
    x1 = x + _dot(mix, wo_ref[...])

    h2 = _rms_norm(x1, n2g_ref[...]).astype(_BF16)
    acc = x1
    def up(c0):
        w = jnp.concatenate([wup_ref[:, c0:c0 + FF_CHUNK], wup_ref[:, D_FF + c0:D_FF + c0 + FF_CHUNK]], axis=1)
        return _dot(h2, w)

    gv_next = up(0)
    for c0 in range(0, D_FF, FF_CHUNK):
        c1 = c0 + FF_CHUNK
        gv = gv_next
        if c1 < D_FF:
            gv_next = up(c1)
        gate = gv[:, :FF_CHUNK].reshape(p, SUBLANES, FF_CHUNK)
        tf_ref[:, :, c0:c1] = gate[p - HIST_3:]
        gc = _conv3(gate, pf_ref, dwf_ref, c0, c1, time_strided).reshape(t, FF_CHUNK)
        hh = (gc * jax.nn.sigmoid(gc) * gv[:, FF_CHUNK:]).astype(_BF16)
        acc = acc + _dot(hh, wdn_ref[c0:c1, :])
    if final:
        acc = _rms_norm(acc, fg_ref[...])
    if permute_out:
        acc = pltpu.einshape("gsd->sgd", acc.reshape(p, SUBLANES, D_MODEL)).reshape(t, D_MODEL)
    xo_ref[...] = acc


def _resident(shape, layer):
    zeros = (0,) * len(shape)
    return pl.BlockSpec((None,) + shape, lambda b, s: (layer,) + zeros, pipeline_mode=pl.Buffered(1))


def _state_spec(groups, ch):
    return pl.BlockSpec((None, groups, SUBLANES, ch), lambda b, i: (b, 0, 0, 0))


def _layer(x, ha, hb, hf, layer, p, t, time_strided, permute_in, permute_out, final):
    n, s, _ = x.shape
    x_spec = pl.BlockSpec((None, t, D_MODEL), lambda b, i: (b, i, 0))
    groups = t // SUBLANES
    n_lt = D_A // LANES
    return pl.pallas_call(
        functools.partial(_layer_kernel, t=t, time_strided=time_strided, permute_in=permute_in,
                          permute_out=permute_out, final=final),
        grid=(n, s // t),
        in_specs=[
            x_spec, _state_spec(HIST_A, D_A), _state_spec(HIST_3, D_B), _state_spec(HIST_3, D_FF),
            _resident((1, D_MODEL), layer), _resident((D_MODEL, D_IN), layer),
            _resident((1, 2 * D_MODEL), layer),
            _resident((n_lt, CONV_A, LANES), layer),
            _resident((1, D_A), layer), _resident((1, D_A), layer), _resident((1, D_A), layer),
            _resident((D_A, D_MODEL), layer), _resident((CONV_B, D_B), layer),
            _resident((D_B, D_MODEL), layer), _resident((D_MODEL, D_MODEL), layer),
            _resident((1, D_MODEL), layer), _resident((D_MODEL, 2 * D_FF), layer),
            _resident((CONV_F, D_FF), layer), _resident((D_FF, D_MODEL), layer),
            pl.BlockSpec((1, D_MODEL), lambda b, i: (0, 0)),
        ],
        out_specs=[x_spec, _state_spec(HIST_A, D_A), _state_spec(HIST_3, D_B), _state_spec(HIST_3, D_FF)],
        out_shape=[
            jax.ShapeDtypeStruct(x.shape, _F32),
            jax.ShapeDtypeStruct((n, HIST_A, SUBLANES, D_A), _F32),
            jax.ShapeDtypeStruct((n, HIST_3, SUBLANES, D_B), _F32),
            jax.ShapeDtypeStruct((n, HIST_3, SUBLANES, D_FF), _F32),
        ],
        scratch_shapes=[
            pltpu.VMEM((n_lt, HIST_A + groups, SUBLANES, LANES), _F32),
            pltpu.VMEM((n_lt, groups, SUBLANES, LANES), _F32),
            pltpu.VMEM((HIST_3, SUBLANES, D_B), _F32),
            pltpu.VMEM((HIST_3, SUBLANES, D_FF), _F32),
        ],
        compiler_params=pltpu.CompilerParams(
            dimension_semantics=("arbitrary", "arbitrary"), vmem_limit_bytes=VMEM_LIMIT_BYTES),
        name="layer",
    )(x, ha, hb, hf, p["norm1_g"], p["w_in"], p["b_gate"], p["dw_a"], p["b_dw_a"],
      p["ln_a_g"], p["ln_a_b"], p["w_a_out"], p["dw_b"], p["w_b_out"], p["w_o"], p["norm2_g"], p["w_up"],
      p["dw_f"], p["w_down"], p["final_g"])


class _Layout:
    def __init__(self, n, s, time_strided):
        self.n, self.s, self.time_strided = n, s, time_strided
        if time_strided:
            self.t = min(s, BLOCK_ROWS)
            self.blocks = n
            assert s % self.t == 0
        else:
            self.t = min(SUBLANES * s, BLOCK_ROWS)
            self.blocks = n // SUBLANES
            assert n % SUBLANES == 0 and (SUBLANES * s) % self.t == 0
        self.p = self.t // SUBLANES
        assert self.p >= HIST_A and self.p % CONV_GROUPS == 0

    def to_rows(self, x):
        c = x.shape[-1]
        if self.time_strided:
            return x
        y = x.reshape(self.blocks, SUBLANES, self.s, c).transpose(0, 2, 1, 3)
        return y.reshape(self.blocks, SUBLANES * self.s, c)

    def from_rows(self, y):
        c = y.shape[-1]
        if self.time_strided:
            return y
        x = y.reshape(self.blocks, self.s, SUBLANES, c).transpose(0, 2, 1, 3)
        return x.reshape(self.n, self.s, c)

    def state_in(self, cache, groups):
        n, w, c = cache.shape
        if self.time_strided:
            st = jnp.zeros((n, groups, SUBLANES, c), cache.dtype)
            return st.at[:, groups - w:, SUBLANES - 1, :].set(cache)
        st = cache.reshape(self.blocks, SUBLANES, w, c).transpose(0, 2, 1, 3)
        return jnp.pad(st, ((0, 0), (groups - w, 0), (0, 0), (0, 0)))

    def state_out(self, tail, w):
        groups, c = tail.shape[1], tail.shape[-1]
        if self.time_strided:
            return tail[:, groups - w:, SUBLANES - 1, :]
        return tail[:, groups - w:].transpose(0, 2, 1, 3).reshape(self.n, w, c)


def kernel(x_prompt, x_sample, cache_conv_a, cache_conv_b, cache_ffn_conv, norm1_g, w_in, b_gate,
           dw_a, b_dw_a, ln_a_g, ln_a_b, w_a_out, dw_b, w_b_out, w_o, norm2_g, w_up, dw_f,
           w_down, final_g):
    depth = w_in.shape[0]
    n_p = x_prompt.shape[0]
    row = lambda a: a.reshape(a.shape[0], 1, a.shape[1])
    p = dict(
        norm1_g=row(norm1_g), w_in=w_in.astype(_BF16), b_gate=row(b_gate),
        dw_a=dw_a.reshape(depth, CONV_A, D_A // LANES, LANES).transpose(0, 2, 1, 3),
        b_dw_a=row(b_dw_a), ln_a_g=row(ln_a_g), ln_a_b=row(ln_a_b), w_a_out=w_a_out.astype(_BF16),
        dw_b=dw_b, w_b_out=w_b_out.astype(_BF16), w_o=w_o.astype(_BF16), norm2_g=row(norm2_g),
        w_up=w_up.astype(_BF16), dw_f=dw_f,
        w_down=w_down.astype(_BF16), final_g=final_g.reshape(1, D_MODEL),
    )
    groups = [
        (_Layout(n_p, x_prompt.shape[1], True), x_prompt,
         lambda l: (jnp.zeros((n_p, CONV_A - 1, D_A), _F32), jnp.zeros((n_p, CONV_B - 1, D_B), _F32),
                    jnp.zeros((n_p, CONV_F - 1, D_FF), _F32))),
        (_Layout(x_sample.shape[0], x_sample.shape[1], False), x_sample,
         lambda l: (cache_conv_a[l], cache_conv_b[l], cache_ffn_conv[l])),
    ]
    ys, new_a, new_b, new_f = [], [], [], []
    for lay, x, caches in groups:
        xr = lay.to_rows(x)
        sa, sb, sf = [], [], []
        for l in range(depth):
            ca, cb, cf = caches(l)
            xr, ta, tb, tf = _layer(xr, lay.state_in(ca, HIST_A), lay.state_in(cb, HIST_3),
                                    lay.state_in(cf, HIST_3), l, p, lay.t, lay.time_strided,
                                    lay.time_strided and l == 0, lay.time_strided and l == depth - 1,
                                    l == depth - 1)
            sa.append(lay.state_out(ta, CONV_A - 1))
            sb.append(lay.state_out(tb, CONV_B - 1))
            sf.append(lay.state_out(tf, CONV_F - 1))
        ys.append(lay.from_rows(xr))
        new_a.append(jnp.stack(sa)); new_b.append(jnp.stack(sb)); new_f.append(jnp.stack(sf))
    return (ys[0], ys[1], new_a[0], new_b[0], new_f[0], new_a[1], new_b[1], new_f[1])
```

```python
import functools

import jax
import jax.numpy as jnp
from jax import lax
from jax.experimental import pallas as pl
from jax.experimental.pallas import tpu as pltpu

D_MODEL = 1024
D_A = 512
D_B = 512
CONV_A = 31
CONV_B = 3
D_FF = 2816
CONV_F = 3
RMS_EPS = 1e-6
LN_EPS = 1e-5

SUBLANES = 8
LANES = 128
HIST_A = 32
HIST_3 = CONV_B - 1
CONV_GROUPS = 8
FF_CHUNK = 256
GATE_COL0 = 2 * D_A + 3 * D_B
D_IN = GATE_COL0 + 2 * D_MODEL
BLOCK_ROWS = 256
VMEM_LIMIT_BYTES = 56 * 1024 * 1024

_F32 = jnp.float32
_BF16 = jnp.bfloat16


def _rms_norm(x, g):
    y = x * lax.rsqrt(jnp.mean(x * x, axis=-1, keepdims=True) + RMS_EPS)
    return y * g


def _dot(a, b):
    return jnp.dot(a, b, preferred_element_type=_F32)


def _history(prev_tail, tail, time_strided):
    if not time_strided:
        return prev_tail
    sub = lax.broadcasted_iota(jnp.int32, tail.shape, 1)
    return jnp.where(sub == 0, pltpu.roll(prev_tail, 1, axis=1), pltpu.roll(tail, 1, axis=1))


def _conv3(v, prev_ref, w_ref, c0, c1, time_strided):
    p = v.shape[0]
    tail = v[p - HIST_3:]
    hist = _history(prev_ref[:, :, c0:c1], tail, time_strided)
    prev_ref[:, :, c0:c1] = tail
    e = jnp.concatenate([hist, v], axis=0)
    out = e[0:p] * w_ref[0:1, c0:c1][None]
    for k in range(1, CONV_B):
        out = out + e[k:k + p] * w_ref[k:k + 1, c0:c1][None]
    return out


def _layer_kernel(x_ref, ha_ref, hb_ref, hf_ref,
                  n1g_ref, win_ref, bgate_ref, dwa_ref, bdwa_ref, lng_ref, lnb_ref, wa_ref,
                  dwb_ref, wb_ref, wo_ref, n2g_ref, wup_ref, dwf_ref, wdn_ref, fg_ref,
                  xo_ref, ta_ref, tb_ref, tf_ref,
                  e_ref, c_ref, pb_ref, pf_ref, x1_ref, h2_ref, *, t, n_t, time_strided,
                  permute_in, permute_out, final):
    j = pl.program_id(0)
    p = t // SUBLANES
    n_lt = D_A // LANES
    cur = j % 2
    prv = 1 - cur

    @pl.when(j == 0)
    def _():
        x1_ref[1] = jnp.zeros((t, D_MODEL), _F32)
        h2_ref[1] = jnp.zeros((t, D_MODEL), _BF16)
        pf_ref[...] = jnp.zeros_like(pf_ref)

    @pl.when(j % n_t == 0)
    def _():
        for q in range(n_lt):
            e_ref[q, p:p + HIST_A] = ha_ref[:, :, q * LANES:(q + 1) * LANES]
        pb_ref[...] = hb_ref[...]

    @pl.when((j % n_t == 1 % n_t) & (j > 0))
    def _():
        pf_ref[...] = hf_ref[...]

    x = x_ref[...]
    if permute_in:
        x = pltpu.einshape("sgd->gsd", x.reshape(SUBLANES, p, D_MODEL)).reshape(t, D_MODEL)
    h = _rms_norm(x, n1g_ref[...]).astype(_BF16)
    za = _dot(h, win_ref[:, 0:2 * D_A])
    gates = jax.nn.sigmoid(_dot(h, win_ref[:, GATE_COL0:]) + bgate_ref[...])
    zb = _dot(h, win_ref[:, 2 * D_A:GATE_COL0])

    u = (za[:, :D_A] * jax.nn.sigmoid(za[:, D_A:])).reshape(p, SUBLANES, D_A)
    tail = u[p - HIST_A:]
    ta_ref[...] = tail
    for q in range(n_lt):
        lanes = slice(q * LANES, (q + 1) * LANES)
        e_ref[q, 0:HIST_A] = _history(e_ref[q, p:p + HIST_A], tail[:, :, lanes], time_strided)
        e_ref[q, HIST_A:HIST_A + p] = u[:, :, lanes]

    cv = (zb[:, D_B:2 * D_B] * zb[:, 2 * D_B:]).reshape(p, SUBLANES, D_B)
    tb_ref[...] = cv[p - HIST_3:]
    v = _conv3(cv, pb_ref, dwb_ref, 0, D_B, time_strided).reshape(t, D_B)
    yb_in = (zb[:, :D_B] * v).astype(_BF16)

    tap0 = HIST_A - (CONV_A - 1)
    n_gb = p // CONV_GROUPS
    n_blocks = n_lt * n_gb
    n_ff = D_FF // FF_CHUNK
    h2_prev = h2_ref.at[prv]

    def f_up(c):
        c0 = c * FF_CHUNK
        w = jnp.concatenate([wup_ref[:, c0:c0 + FF_CHUNK], wup_ref[:, D_FF + c0:D_FF + c0 + FF_CHUNK]], axis=1)
        return _dot(h2_prev[...], w)

    def conv_block(blk, pace):
        q, g0 = blk // n_gb, (blk % n_gb) * CONV_GROUPS
        zero = (pltpu.bitcast(pace[0:SUBLANES, 0:LANES], jnp.uint32) >> 16) >> 16
        win = e_ref[q, g0:g0 + CONV_GROUPS + HIST_A]
        acc = None
        for k in range(CONV_A):
            w = jnp.broadcast_to(dwa_ref[q, k:k + 1, :], (SUBLANES, LANES))
            w = pltpu.bitcast(pltpu.bitcast(w, jnp.uint32) | zero, _F32)
            term = win[tap0 + k:tap0 + k + CONV_GROUPS] * w[None]
            acc = term if acc is None else acc + term
        c_ref[q, g0:g0 + CONV_GROUPS] = acc

    facc = x1_ref[prv]
    gv_next = f_up(0)
    done = 0
    for c in range(n_ff):
        c0, c1 = c * FF_CHUNK, (c + 1) * FF_CHUNK
        gv = gv_next
        if c + 1 < n_ff:
            gv_next = f_up(c + 1)
        upto = min(n_blocks, ((c + 1) * n_blocks + n_ff - 2) // (n_ff - 1))
        for blk in range(done, upto):
            conv_block(blk, gv)
        done = upto
        gate = gv[:, :FF_CHUNK].reshape(p, SUBLANES, FF_CHUNK)
        tf_ref[:, :, c0:c1] = gate[p - HIST_3:]
        gc = _conv3(gate, pf_ref, dwf_ref, c0, c1, time_strided).reshape(t, FF_CHUNK)
        hh = (gc * jax.nn.sigmoid(gc) * gv[:, FF_CHUNK:]).astype(_BF16)
        facc = facc + _dot(hh, wdn_ref[c0:c1, :])
    assert done == n_blocks

    y_b = _dot(yb_in, wb_ref[...])
    conv = jnp.concatenate([c_ref[q] for q in range(n_lt)], axis=-1).reshape(t, D_A) + bdwa_ref[...]
    mu = jnp.mean(conv, axis=-1, keepdims=True)
    xc = conv - mu
    var = jnp.mean(xc * xc, axis=-1, keepdims=True)
    ya = xc * lax.rsqrt(var + LN_EPS) * lng_ref[...] + lnb_ref[...]
    y_a = _dot((ya * jax.nn.sigmoid(ya)).astype(_BF16), wa_ref[...])
    mix = (gates[:, :D_MODEL] * y_a + gates[:, D_MODEL:] * y_b).astype(_BF16)
    x1 = x + _dot(mix, wo_ref[...])
    x1_ref[cur] = x1
    h2_ref[cur] = _rms_norm(x1, n2g_ref[...]).astype(_BF16)

    if final:
        facc = _rms_norm(facc, fg_ref[...])
    if permute_out:
        facc = pltpu.einshape("gsd->sgd", facc.reshape(p, SUBLANES, D_MODEL)).reshape(t, D_MODEL)
    xo_ref[...] = facc


def _resident(shape, layer):
    zeros = (0,) * len(shape)
    return pl.BlockSpec((None,) + shape, lambda j: (layer,) + zeros, pipeline_mode=pl.Buffered(1))


def _layer(x, ha, hb, hf, layer, p, t, time_strided, permute_in, permute_out, final):
    n, s, _ = x.shape
    n_t = s // t
    n_tiles = n * n_t
    groups = t // SUBLANES
    n_lt = D_A // LANES
    mix_tile = lambda j: jnp.minimum(j, n_tiles - 1)
    ffn_tile = lambda j: jnp.maximum(j - 1, 0)
    x_in = pl.BlockSpec((None, t, D_MODEL), lambda j: (mix_tile(j) // n_t, mix_tile(j) % n_t, 0))
    x_out = pl.BlockSpec((None, t, D_MODEL), lambda j: (ffn_tile(j) // n_t, ffn_tile(j) % n_t, 0))
    state = lambda g, ch, tile: pl.BlockSpec((None, g, SUBLANES, ch), lambda j: (tile(j) // n_t, 0, 0, 0))
    return pl.pallas_call(
        functools.partial(_layer_kernel, t=t, n_t=n_t, time_strided=time_strided, permute_in=permute_in,
                          permute_out=permute_out, final=final),
        grid=(n_tiles + 1,),
        in_specs=[
            x_in, state(HIST_A, D_A, mix_tile), state(HIST_3, D_B, mix_tile), state(HIST_3, D_FF, ffn_tile),
            _resident((1, D_MODEL), layer), _resident((D_MODEL, D_IN), layer),
            _resident((1, 2 * D_MODEL), layer),
            _resident((n_lt, CONV_A, LANES), layer),
            _resident((1, D_A), layer), _resident((1, D_A), layer), _resident((1, D_A), layer),
            _resident((D_A, D_MODEL), layer), _resident((CONV_B, D_B), layer),
            _resident((D_B, D_MODEL), layer), _resident((D_MODEL, D_MODEL), layer),
            _resident((1, D_MODEL), layer), _resident((D_MODEL, 2 * D_FF), layer),
            _resident((CONV_F, D_FF), layer), _resident((D_FF, D_MODEL), layer),
            pl.BlockSpec((1, D_MODEL), lambda j: (0, 0)),
        ],
        out_specs=[x_out, state(HIST_A, D_A, mix_tile), state(HIST_3, D_B, mix_tile),
                   state(HIST_3, D_FF, ffn_tile)],
        out_shape=[
            jax.ShapeDtypeStruct(x.shape, _F32),
            jax.ShapeDtypeStruct((n, HIST_A, SUBLANES, D_A), _F32),
            jax.ShapeDtypeStruct((n, HIST_3, SUBLANES, D_B), _F32),
            jax.ShapeDtypeStruct((n, HIST_3, SUBLANES, D_FF), _F32),
        ],
        scratch_shapes=[
            pltpu.VMEM((n_lt, HIST_A + groups, SUBLANES, LANES), _F32),
            pltpu.VMEM((n_lt, groups, SUBLANES, LANES), _F32),
            pltpu.VMEM((HIST_3, SUBLANES, D_B), _F32),
            pltpu.VMEM((HIST_3, SUBLANES, D_FF), _F32),
            pltpu.VMEM((2, t, D_MODEL), _F32),
            pltpu.VMEM((2, t, D_MODEL), _BF16),
        ],
        compiler_params=pltpu.CompilerParams(
            dimension_semantics=("arbitrary",), vmem_limit_bytes=VMEM_LIMIT_BYTES),
        name="layer",
    )(x, ha, hb, hf, p["norm1_g"], p["w_in"], p["b_gate"], p["dw_a"], p["b_dw_a"],
      p["ln_a_g"], p["ln_a_b"], p["w_a_out"], p["dw_b"], p["w_b_out"], p["w_o"], p["norm2_g"], p["w_up"],
      p["dw_f"], p["w_down"], p["final_g"])


class _Layout:
    def __init__(self, n, s, time_strided):
        self.n, self.s, self.time_strided = n, s, time_strided
        if time_strided:
            self.t = min(s, BLOCK_ROWS)
            self.blocks = n
            assert s % self.t == 0
        else:
            self.t = min(SUBLANES * s, BLOCK_ROWS)
            self.blocks = n // SUBLANES
            assert n % SUBLANES == 0 and (SUBLANES * s) % self.t == 0
        self.p = self.t // SUBLANES
        assert self.p >= HIST_A and self.p % CONV_GROUPS == 0

    def to_rows(self, x):
        c = x.shape[-1]
        if self.time_strided:
            return x
        y = x.reshape(self.blocks, SUBLANES, self.s, c).transpose(0, 2, 1, 3)
        return y.reshape(self.blocks, SUBLANES * self.s, c)

    def from_rows(self, y):
        c = y.shape[-1]
        if self.time_strided:
            return y
        x = y.reshape(self.blocks, self.s, SUBLANES, c).transpose(0, 2, 1, 3)
        return x.reshape(self.n, self.s, c)

    def state_in(self, cache, groups):
        n, w, c = cache.shape
        if self.time_strided:
            st = jnp.zeros((n, groups, SUBLANES, c), cache.dtype)
            return st.at[:, groups - w:, SUBLANES - 1, :].set(cache)
        st = cache.reshape(self.blocks, SUBLANES, w, c).transpose(0, 2, 1, 3)
        return jnp.pad(st, ((0, 0), (groups - w, 0), (0, 0), (0, 0)))

    def state_out(self, tail, w):
        groups, c = tail.shape[1], tail.shape[-1]
        if self.time_strided:
            return tail[:, groups - w:, SUBLANES - 1, :]
        return tail[:, groups - w:].transpose(0, 2, 1, 3).reshape(self.n, w, c)


def kernel(x_prompt, x_sample, cache_conv_a, cache_conv_b, cache_ffn_conv, norm1_g, w_in, b_gate,
           dw_a, b_dw_a, ln_a_g, ln_a_b, w_a_out, dw_b, w_b_out, w_o, norm2_g, w_up, dw_f,
           w_down, final_g):
    depth = w_in.shape[0]
    n_p = x_prompt.shape[0]
    row = lambda a: a.reshape(a.shape[0], 1, a.shape[1])
    p = dict(
        norm1_g=row(norm1_g), w_in=w_in.astype(_BF16), b_gate=row(b_gate),
        dw_a=dw_a.reshape(depth, CONV_A, D_A // LANES, LANES).transpose(0, 2, 1, 3),
        b_dw_a=row(b_dw_a), ln_a_g=row(ln_a_g), ln_a_b=row(ln_a_b), w_a_out=w_a_out.astype(_BF16),
        dw_b=dw_b, w_b_out=w_b_out.astype(_BF16), w_o=w_o.astype(_BF16), norm2_g=row(norm2_g),
        w_up=w_up.astype(_BF16), dw_f=dw_f,
        w_down=w_down.astype(_BF16), final_g=final_g.reshape(1, D_MODEL),
    )
    groups = [
        (_Layout(n_p, x_prompt.shape[1], True), x_prompt,
         lambda l: (jnp.zeros((n_p, CONV_A - 1, D_A), _F32), jnp.zeros((n_p, CONV_B - 1, D_B), _F32),
                    jnp.zeros((n_p, CONV_F - 1, D_FF), _F32))),
        (_Layout(x_sample.shape[0], x_sample.shape[1], False), x_sample,
         lambda l: (cache_conv_a[l], cache_conv_b[l], cache_ffn_conv[l])),
    ]
    ys, new_a, new_b, new_f = [], [], [], []
    for lay, x, caches in groups:
        xr = lay.to_rows(x)
        sa, sb, sf = [], [], []
        for l in range(depth):
            ca, cb, cf = caches(l)
            xr, ta, tb, tf = _layer(xr, lay.state_in(ca, HIST_A), lay.state_in(cb, HIST_3),
                                    lay.state_in(cf, HIST_3), l, p, lay.t, lay.time_strided,
                                    lay.time_strided and l == 0, lay.time_strided and l == depth - 1,
                                    l == depth - 1)
            sa.append(lay.state_out(ta, CONV_A - 1))
            sb.append(lay.state_out(tb, CONV_B - 1))
            sf.append(lay.state_out(tf, CONV_F - 1))
        ys.append(lay.from_rows(xr))
        new_a.append(jnp.stack(sa)); new_b.append(jnp.stack(sb)); new_f.append(jnp.stack(sf))
    return (ys[0], ys[1], new_a[0], new_b[0], new_f[0], new_a[1], new_b[1], new_f[1])
```

```python
import functools

import jax
import jax.numpy as jnp
from jax import lax
from jax.experimental import pallas as pl
from jax.experimental.pallas import tpu as pltpu

D_MODEL = 1024
D_A = 512
D_B = 512
CONV_A = 31
CONV_B = 3
D_FF = 2816
CONV_F = 3
RMS_EPS = 1e-6
LN_EPS = 1e-5

SUBLANES = 8
LANES = 128
HIST_A = 32
HIST_3 = CONV_B - 1
CONV_GROUPS = 8
FF_CHUNK = 256
FFN_LEAD_CHUNKS = 2
GATE_COL0 = 2 * D_A + 3 * D_B
D_IN = GATE_COL0 + 2 * D_MODEL
BLOCK_ROWS = 256
VMEM_LIMIT_BYTES = 56 * 1024 * 1024

_F32 = jnp.float32
_BF16 = jnp.bfloat16


def _rms_norm(x, g):
    y = x * lax.rsqrt(jnp.mean(x * x, axis=-1, keepdims=True) + RMS_EPS)
    return y * g


def _dot(a, b):
    return jnp.dot(a, b, preferred_element_type=_F32)


def _history(prev_tail, tail, time_strided):
    if not time_strided:
        return prev_tail
    sub = lax.broadcasted_iota(jnp.int32, tail.shape, 1)
    return jnp.where(sub == 0, pltpu.roll(prev_tail, 1, axis=1), pltpu.roll(tail, 1, axis=1))


def _conv3(v, prev_ref, w_ref, c0, c1, time_strided):
    p = v.shape[0]
    tail = v[p - HIST_3:]
    hist = _history(prev_ref[:, :, c0:c1], tail, time_strided)
    prev_ref[:, :, c0:c1] = tail
    e = jnp.concatenate([hist, v], axis=0)
    out = e[0:p] * w_ref[0:1, c0:c1][None]
    for k in range(1, CONV_B):
        out = out + e[k:k + p] * w_ref[k:k + 1, c0:c1][None]
    return out


def _layer_kernel(x_ref, ha_ref, hb_ref, hf_ref,
                  n1g_ref, win_ref, bgate_ref, dwa_ref, bdwa_ref, lng_ref, lnb_ref, wa_ref,
                  dwb_ref, wb_ref, wo_ref, n2g_ref, wup_ref, dwf_ref, wdn_ref, fg_ref,
                  xo_ref, ta_ref, tb_ref, tf_ref,
                  e_ref, c_ref, pb_ref, pf_ref, x1_ref, h2_ref, *, t, n_t, time_strided,
                  permute_in, permute_out, final):
    j = pl.program_id(0)
    p = t // SUBLANES
    n_lt = D_A // LANES
    cur = j % 2
    prv = 1 - cur

    @pl.when(j == 0)
    def _():
        x1_ref[1] = jnp.zeros((t, D_MODEL), _F32)
        h2_ref[1] = jnp.zeros((t, D_MODEL), _BF16)
        pf_ref[...] = jnp.zeros_like(pf_ref)

    @pl.when(j % n_t == 0)
    def _():
        for q in range(n_lt):
            e_ref[q, p:p + HIST_A] = ha_ref[:, :, q * LANES:(q + 1) * LANES]
        pb_ref[...] = hb_ref[...]

    @pl.when((j % n_t == 1 % n_t) & (j > 0))
    def _():
        pf_ref[...] = hf_ref[...]

    tap0 = HIST_A - (CONV_A - 1)
    n_gb = p // CONV_GROUPS
    n_blocks = n_lt * n_gb
    n_ff = D_FF // FF_CHUNK
    h2_prev = h2_ref.at[prv]

    def f_up(c):
        c0 = c * FF_CHUNK
        w = jnp.concatenate([wup_ref[:, c0:c0 + FF_CHUNK], wup_ref[:, D_FF + c0:D_FF + c0 + FF_CHUNK]], axis=1)
        return _dot(h2_prev[...], w)

    def conv_block(blk, pace):
        q, g0 = blk // n_gb, (blk % n_gb) * CONV_GROUPS
        zero = (pltpu.bitcast(pace[0:SUBLANES, 0:LANES], jnp.uint32) >> 16) >> 16
        win = e_ref[q, g0:g0 + CONV_GROUPS + HIST_A]
        acc = None
        for k in range(CONV_A):
            w = jnp.broadcast_to(dwa_ref[q, k:k + 1, :], (SUBLANES, LANES))
            w = pltpu.bitcast(pltpu.bitcast(w, jnp.uint32) | zero, _F32)
            term = win[tap0 + k:tap0 + k + CONV_GROUPS] * w[None]
            acc = term if acc is None else acc + term
        c_ref[q, g0:g0 + CONV_GROUPS] = acc

    state = {"facc": x1_ref[prv], "gv_next": f_up(0), "done": 0}

    def ffn_chunk(c, n_conv):
        c0, c1 = c * FF_CHUNK, (c + 1) * FF_CHUNK
        gv = state["gv_next"]
        if c + 1 < n_ff:
            state["gv_next"] = f_up(c + 1)
        for blk in range(state["done"], min(n_blocks, state["done"] + n_conv)):
            conv_block(blk, gv)
        state["done"] = min(n_blocks, state["done"] + n_conv)
        gate = gv[:, :FF_CHUNK].reshape(p, SUBLANES, FF_CHUNK)
        tf_ref[:, :, c0:c1] = gate[p - HIST_3:]
        gc = _conv3(gate, pf_ref, dwf_ref, c0, c1, time_strided).reshape(t, FF_CHUNK)
        hh = (gc * jax.nn.sigmoid(gc) * gv[:, FF_CHUNK:]).astype(_BF16)
        state["facc"] = state["facc"] + _dot(hh, wdn_ref[c0:c1, :])

    for c in range(FFN_LEAD_CHUNKS):
        ffn_chunk(c, 0)

    x = x_ref[...]
    if permute_in:
        x = pltpu.einshape("sgd->gsd", x.reshape(SUBLANES, p, D_MODEL)).reshape(t, D_MODEL)
    h = _rms_norm(x, n1g_ref[...]).astype(_BF16)
    za = _dot(h, win_ref[:, 0:2 * D_A])
    gates = jax.nn.sigmoid(_dot(h, win_ref[:, GATE_COL0:]) + bgate_ref[...])
    zb = _dot(h, win_ref[:, 2 * D_A:GATE_COL0])

    u = (za[:, :D_A] * jax.nn.sigmoid(za[:, D_A:])).reshape(p, SUBLANES, D_A)
    tail = u[p - HIST_A:]
    ta_ref[...] = tail
    for q in range(n_lt):
        lanes = slice(q * LANES, (q + 1) * LANES)
        e_ref[q, 0:HIST_A] = _history(e_ref[q, p:p + HIST_A], tail[:, :, lanes], time_strided)
        e_ref[q, HIST_A:HIST_A + p] = u[:, :, lanes]

    cv = (zb[:, D_B:2 * D_B] * zb[:, 2 * D_B:]).reshape(p, SUBLANES, D_B)
    tb_ref[...] = cv[p - HIST_3:]
    v = _conv3(cv, pb_ref, dwb_ref, 0, D_B, time_strided).reshape(t, D_B)
    yb_in = (zb[:, :D_B] * v).astype(_BF16)

    per_chunk = -(-n_blocks // (n_ff - 1 - FFN_LEAD_CHUNKS))
    for c in range(FFN_LEAD_CHUNKS, n_ff):
        ffn_chunk(c, per_chunk)
    assert state["done"] == n_blocks
    facc = state["facc"]

    y_b = _dot(yb_in, wb_ref[...])
    conv = jnp.concatenate([c_ref[q] for q in range(n_lt)], axis=-1).reshape(t, D_A) + bdwa_ref[...]
    mu = jnp.mean(conv, axis=-1, keepdims=True)
    xc = conv - mu
    var = jnp.mean(xc * xc, axis=-1, keepdims=True)
    ya = xc * lax.rsqrt(var + LN_EPS) * lng_ref[...] + lnb_ref[...]
    y_a = _dot((ya * jax.nn.sigmoid(ya)).astype(_BF16), wa_ref[...])
    mix = (gates[:, :D_MODEL] * y_a + gates[:, D_MODEL:] * y_b).astype(_BF16)
    x1 = x + _dot(mix, wo_ref[...])
    x1_ref[cur] = x1
    h2_ref[cur] = _rms_norm(x1, n2g_ref[...]).astype(_BF16)

    if final:
        facc = _rms_norm(facc, fg_ref[...])
    if permute_out:
        facc = pltpu.einshape("gsd->sgd", facc.reshape(p, SUBLANES, D_MODEL)).reshape(t, D_MODEL)
    xo_ref[...] = facc


def _resident(shape, layer):
    zeros = (0,) * len(shape)
    return pl.BlockSpec((None,) + shape, lambda j: (layer,) + zeros, pipeline_mode=pl.Buffered(1))


def _layer(x, ha, hb, hf, layer, p, t, time_strided, permute_in, permute_out, final):
    n, s, _ = x.shape
    n_t = s // t
    n_tiles = n * n_t
    groups = t // SUBLANES
    n_lt = D_A // LANES
    mix_tile = lambda j: jnp.minimum(j, n_tiles - 1)
    ffn_tile = lambda j: jnp.maximum(j - 1, 0)
    x_in = pl.BlockSpec((None, t, D_MODEL), lambda j: (mix_tile(j) // n_t, mix_tile(j) % n_t, 0))
    x_out = pl.BlockSpec((None, t, D_MODEL), lambda j: (ffn_tile(j) // n_t, ffn_tile(j) % n_t, 0))
    state = lambda g, ch, tile: pl.BlockSpec((None, g, SUBLANES, ch), lambda j: (tile(j) // n_t, 0, 0, 0))
    return pl.pallas_call(
        functools.partial(_layer_kernel, t=t, n_t=n_t, time_strided=time_strided, permute_in=permute_in,
                          permute_out=permute_out, final=final),
        grid=(n_tiles + 1,),
        in_specs=[
            x_in, state(HIST_A, D_A, mix_tile), state(HIST_3, D_B, mix_tile), state(HIST_3, D_FF, ffn_tile),
            _resident((1, D_MODEL), layer), _resident((D_MODEL, D_IN), layer),
            _resident((1, 2 * D_MODEL), layer),
            _resident((n_lt, CONV_A, LANES), layer),
            _resident((1, D_A), layer), _resident((1, D_A), layer), _resident((1, D_A), layer),
            _resident((D_A, D_MODEL), layer), _resident((CONV_B, D_B), layer),
            _resident((D_B, D_MODEL), layer), _resident((D_MODEL, D_MODEL), layer),
            _resident((1, D_MODEL), layer), _resident((D_MODEL, 2 * D_FF), layer),
            _resident((CONV_F, D_FF), layer), _resident((D_FF, D_MODEL), layer),
            pl.BlockSpec((1, D_MODEL), lambda j: (0, 0)),
        ],
        out_specs=[x_out, state(HIST_A, D_A, mix_tile), state(HIST_3, D_B, mix_tile),
                   state(HIST_3, D_FF, ffn_tile)],
        out_shape=[
            jax.ShapeDtypeStruct(x.shape, _F32),
            jax.ShapeDtypeStruct((n, HIST_A, SUBLANES, D_A), _F32),
            jax.ShapeDtypeStruct((n, HIST_3, SUBLANES, D_B), _F32),
            jax.ShapeDtypeStruct((n, HIST_3, SUBLANES, D_FF), _F32),
        ],
        scratch_shapes=[
            pltpu.VMEM((n_lt, HIST_A + groups, SUBLANES, LANES), _F32),
            pltpu.VMEM((n_lt, groups, SUBLANES, LANES), _F32),
            pltpu.VMEM((HIST_3, SUBLANES, D_B), _F32),
            pltpu.VMEM((HIST_3, SUBLANES, D_FF), _F32),
            pltpu.VMEM((2, t, D_MODEL), _F32),
            pltpu.VMEM((2, t, D_MODEL), _BF16),
        ],
        compiler_params=pltpu.CompilerParams(
            dimension_semantics=("arbitrary",), vmem_limit_bytes=VMEM_LIMIT_BYTES),
        name="layer",
    )(x, ha, hb, hf, p["norm1_g"], p["w_in"], p["b_gate"], p["dw_a"], p["b_dw_a"],
      p["ln_a_g"], p["ln_a_b"], p["w_a_out"], p["dw_b"], p["w_b_out"], p["w_o"], p["norm2_g"], p["w_up"],
      p["dw_f"], p["w_down"], p["final_g"])


class _Layout:
    def __init__(self, n, s, time_strided):
        self.n, self.s, self.time_strided = n, s, time_strided
        if time_strided:
            self.t = min(s, BLOCK_ROWS)
            self.blocks = n
            assert s % self.t == 0
        else:
            self.t = min(SUBLANES * s, BLOCK_ROWS)
            self.blocks = n // SUBLANES
            assert n % SUBLANES == 0 and (SUBLANES * s) % self.t == 0
        self.p = self.t // SUBLANES
        assert self.p >= HIST_A and self.p % CONV_GROUPS == 0

    def to_rows(self, x):
        c = x.shape[-1]
        if self.time_strided:
            return x
        y = x.reshape(self.blocks, SUBLANES, self.s, c).transpose(0, 2, 1, 3)
        return y.reshape(self.blocks, SUBLANES * self.s, c)

    def from_rows(self, y):
        c = y.shape[-1]
        if self.time_strided:
            return y
        x = y.reshape(self.blocks, self.s, SUBLANES, c).transpose(0, 2, 1, 3)
        return x.reshape(self.n, self.s, c)

    def state_in(self, cache, groups):
        n, w, c = cache.shape
        if self.time_strided:
            st = jnp.zeros((n, groups, SUBLANES, c), cache.dtype)
            return st.at[:, groups - w:, SUBLANES - 1, :].set(cache)
        st = cache.reshape(self.blocks, SUBLANES, w, c).transpose(0, 2, 1, 3)
        return jnp.pad(st, ((0, 0), (groups - w, 0), (0, 0), (0, 0)))

    def state_out(self, tail, w):
        groups, c = tail.shape[1], tail.shape[-1]
        if self.time_strided:
            return tail[:, groups - w:, SUBLANES - 1, :]
        return tail[:, groups - w:].transpose(0, 2, 1, 3).reshape(self.n, w, c)


def kernel(x_prompt, x_sample, cache_conv_a, cache_conv_b, cache_ffn_conv, norm1_g, w_in, b_gate,
           dw_a, b_dw_a, ln_a_g, ln_a_b, w_a_out, dw_b, w_b_out, w_o, norm2_g, w_up, dw_f,
           w_down, final_g):
    depth = w_in.shape[0]
    n_p = x_prompt.shape[0]
    row = lambda a: a.reshape(a.shape[0], 1, a.shape[1])
    p = dict(
        norm1_g=row(norm1_g), w_in=w_in.astype(_BF16), b_gate=row(b_gate),
        dw_a=dw_a.reshape(depth, CONV_A, D_A // LANES, LANES).transpose(0, 2, 1, 3),
        b_dw_a=row(b_dw_a), ln_a_g=row(ln_a_g), ln_a_b=row(ln_a_b), w_a_out=w_a_out.astype(_BF16),
        dw_b=dw_b, w_b_out=w_b_out.astype(_BF16), w_o=w_o.astype(_BF16), norm2_g=row(norm2_g),
        w_up=w_up.astype(_BF16), dw_f=dw_f,
        w_down=w_down.astype(_BF16), final_g=final_g.reshape(1, D_MODEL),
    )
    groups = [
        (_Layout(n_p, x_prompt.shape[1], True), x_prompt,
         lambda l: (jnp.zeros((n_p, CONV_A - 1, D_A), _F32), jnp.zeros((n_p, CONV_B - 1, D_B), _F32),
                    jnp.zeros((n_p, CONV_F - 1, D_FF), _F32))),
        (_Layout(x_sample.shape[0], x_sample.shape[1], False), x_sample,
         lambda l: (cache_conv_a[l], cache_conv_b[l], cache_ffn_conv[l])),
    ]
    ys, new_a, new_b, new_f = [], [], [], []
    for lay, x, caches in groups:
        xr = lay.to_rows(x)
        sa, sb, sf = [], [], []
        for l in range(depth):
            ca, cb, cf = caches(l)
            xr, ta, tb, tf = _layer(xr, lay.state_in(ca, HIST_A), lay.state_in(cb, HIST_3),
                                    lay.state_in(cf, HIST_3), l, p, lay.t, lay.time_strided,
                                    lay.time_strided and l == 0, lay.time_strided and l == depth - 1,
                                    l == depth - 1)
            sa.append(lay.state_out(ta, CONV_A - 1))
            sb.append(lay.state_out(tb, CONV_B - 1))
            sf.append(lay.state_out(tf, CONV_F - 1))
        ys.append(lay.from_rows(xr))
        new_a.append(jnp.stack(sa)); new_b.append(jnp.stack(sb)); new_f.append(jnp.stack(sf))
    return (ys[0], ys[1], new_a[0], new_b[0], new_f[0], new_a[1], new_b[1], new_f[1])
```

```python
import functools

import jax
import jax.numpy as jnp
from jax import lax
from jax.experimental import pallas as pl
from jax.experimental.pallas import tpu as pltpu

D_MODEL = 1024
D_A = 512
D_B = 512
CONV_A = 31
CONV_B = 3
D_FF = 2816
CONV_F = 3
RMS_EPS = 1e-6
LN_EPS = 1e-5

SUBLANES = 8
LANES = 128
HIST_A = 32
HIST_3 = CONV_B - 1
CONV_GROUPS = 8
FF_CHUNK = 256
FFN_LEAD_CHUNKS = 3
GATE_COL0 = 2 * D_A + 3 * D_B
D_IN = GATE_COL0 + 2 * D_MODEL
BLOCK_ROWS = 256
VMEM_LIMIT_BYTES = 56 * 1024 * 1024

_F32 = jnp.float32
_BF16 = jnp.bfloat16


def _rms_norm(x, g):
    y = x * lax.rsqrt(jnp.mean(x * x, axis=-1, keepdims=True) + RMS_EPS)
    return y * g


def _dot(a, b):
    return jnp.dot(a, b, preferred_element_type=_F32)


def _history(prev_tail, tail, time_strided):
    if not time_strided:
        return prev_tail
    sub = lax.broadcasted_iota(jnp.int32, tail.shape, 1)
    return jnp.where(sub == 0, pltpu.roll(prev_tail, 1, axis=1), pltpu.roll(tail, 1, axis=1))


def _conv3(v, prev_ref, w_ref, c0, c1, time_strided):
    p = v.shape[0]
    tail = v[p - HIST_3:]
    hist = _history(prev_ref[:, :, c0:c1], tail, time_strided)
    prev_ref[:, :, c0:c1] = tail
    e = jnp.concatenate([hist, v], axis=0)
    out = e[0:p] * w_ref[0:1, c0:c1][None]
    for k in range(1, CONV_B):
        out = out + e[k:k + p] * w_ref[k:k + 1, c0:c1][None]
    return out


def _layer_kernel(x_ref, ha_ref, hb_ref, hf_ref,
                  n1g_ref, win_ref, bgate_ref, dwa_ref, bdwa_ref, lng_ref, lnb_ref, wa_ref,
                  dwb_ref, wb_ref, wo_ref, n2g_ref, wup_ref, dwf_ref, wdn_ref, fg_ref,
                  xo_ref, ta_ref, tb_ref, tf_ref,
                  e_ref, c_ref, pb_ref, pf_ref, x1_ref, h2_ref, *, t, n_t, time_strided,
                  permute_in, permute_out, final):
    j = pl.program_id(0)
    p = t // SUBLANES
    n_lt = D_A // LANES
    cur = j % 2
    prv = 1 - cur

    @pl.when(j == 0)
    def _():
        x1_ref[1] = jnp.zeros((t, D_MODEL), _F32)
        h2_ref[1] = jnp.zeros((t, D_MODEL), _BF16)
        pf_ref[...] = jnp.zeros_like(pf_ref)

    @pl.when(j % n_t == 0)
    def _():
        for q in range(n_lt):
            e_ref[q, p:p + HIST_A] = ha_ref[:, :, q * LANES:(q + 1) * LANES]
        pb_ref[...] = hb_ref[...]

    @pl.when((j % n_t == 1 % n_t) & (j > 0))
    def _():
        pf_ref[...] = hf_ref[...]

    tap0 = HIST_A - (CONV_A - 1)
    n_gb = p // CONV_GROUPS
    n_blocks = n_lt * n_gb
    n_ff = D_FF // FF_CHUNK
    h2_prev = h2_ref.at[prv]

    def f_up(c):
        c0 = c * FF_CHUNK
        w = jnp.concatenate([wup_ref[:, c0:c0 + FF_CHUNK], wup_ref[:, D_FF + c0:D_FF + c0 + FF_CHUNK]], axis=1)
        return _dot(h2_prev[...], w)

    def conv_block(blk, pace):
        q, g0 = blk // n_gb, (blk % n_gb) * CONV_GROUPS
        zero = (pltpu.bitcast(pace[0:SUBLANES, 0:LANES], jnp.uint32) >> 16) >> 16
        win = e_ref[q, g0:g0 + CONV_GROUPS + HIST_A]
        acc = None
        for k in range(CONV_A):
            w = jnp.broadcast_to(dwa_ref[q, k:k + 1, :], (SUBLANES, LANES))
            w = pltpu.bitcast(pltpu.bitcast(w, jnp.uint32) | zero, _F32)
            term = win[tap0 + k:tap0 + k + CONV_GROUPS] * w[None]
            acc = term if acc is None else acc + term
        c_ref[q, g0:g0 + CONV_GROUPS] = acc

    state = {"facc": x1_ref[prv], "gv_next": f_up(0), "done": 0}

    def ffn_chunk(c, n_conv):
        c0, c1 = c * FF_CHUNK, (c + 1) * FF_CHUNK
        gv = state["gv_next"]
        if c + 1 < n_ff:
            state["gv_next"] = f_up(c + 1)
        for blk in range(state["done"], min(n_blocks, state["done"] + n_conv)):
            conv_block(blk, gv)
        state["done"] = min(n_blocks, state["done"] + n_conv)
        gate = gv[:, :FF_CHUNK].reshape(p, SUBLANES, FF_CHUNK)
        tf_ref[:, :, c0:c1] = gate[p - HIST_3:]
        gc = _conv3(gate, pf_ref, dwf_ref, c0, c1, time_strided).reshape(t, FF_CHUNK)
        hh = (gc * jax.nn.sigmoid(gc) * gv[:, FF_CHUNK:]).astype(_BF16)
        state["facc"] = state["facc"] + _dot(hh, wdn_ref[c0:c1, :])

    for c in range(FFN_LEAD_CHUNKS):
        ffn_chunk(c, 0)

    x = x_ref[...]
    if permute_in:
        x = pltpu.einshape("sgd->gsd", x.reshape(SUBLANES, p, D_MODEL)).reshape(t, D_MODEL)
    h = _rms_norm(x, n1g_ref[...]).astype(_BF16)
    za = _dot(h, win_ref[:, 0:2 * D_A])
    gates = jax.nn.sigmoid(_dot(h, win_ref[:, GATE_COL0:]) + bgate_ref[...])
    zb = _dot(h, win_ref[:, 2 * D_A:GATE_COL0])

    u = (za[:, :D_A] * jax.nn.sigmoid(za[:, D_A:])).reshape(p, SUBLANES, D_A)
    tail = u[p - HIST_A:]
    ta_ref[...] = tail
    for q in range(n_lt):
        lanes = slice(q * LANES, (q + 1) * LANES)
        e_ref[q, 0:HIST_A] = _history(e_ref[q, p:p + HIST_A], tail[:, :, lanes], time_strided)
        e_ref[q, HIST_A:HIST_A + p] = u[:, :, lanes]

    cv = (zb[:, D_B:2 * D_B] * zb[:, 2 * D_B:]).reshape(p, SUBLANES, D_B)
    tb_ref[...] = cv[p - HIST_3:]
    v = _conv3(cv, pb_ref, dwb_ref, 0, D_B, time_strided).reshape(t, D_B)
    yb_in = (zb[:, :D_B] * v).astype(_BF16)

    per_chunk = -(-n_blocks // (n_ff - 1 - FFN_LEAD_CHUNKS))
    for c in range(FFN_LEAD_CHUNKS, n_ff):
        ffn_chunk(c, per_chunk)
    assert state["done"] == n_blocks
    facc = state["facc"]

    y_b = _dot(yb_in, wb_ref[...])
    conv = jnp.concatenate([c_ref[q] for q in range(n_lt)], axis=-1).reshape(t, D_A) + bdwa_ref[...]
    mu = jnp.mean(conv, axis=-1, keepdims=True)
    xc = conv - mu
    var = jnp.mean(xc * xc, axis=-1, keepdims=True)
    ya = xc * lax.rsqrt(var + LN_EPS) * lng_ref[...] + lnb_ref[...]
    y_a = _dot((ya * jax.nn.sigmoid(ya)).astype(_BF16), wa_ref[...])
    mix = (gates[:, :D_MODEL] * y_a + gates[:, D_MODEL:] * y_b).astype(_BF16)
    x1 = x + _dot(mix, wo_ref[...])
    x1_ref[cur] = x1
    h2_ref[cur] = _rms_norm(x1, n2g_ref[...]).astype(_BF16)

    if final:
        facc = _rms_norm(facc, fg_ref[...])
    if permute_out:
        facc = pltpu.einshape("gsd->sgd", facc.reshape(p, SUBLANES, D_MODEL)).reshape(t, D_MODEL)
    xo_ref[...] = facc


def _resident(shape, layer):
    zeros = (0,) * len(shape)
    return pl.BlockSpec((None,) + shape, lambda j: (layer,) + zeros, pipeline_mode=pl.Buffered(1))


def _layer(x, ha, hb, hf, layer, p, t, time_strided, permute_in, permute_out, final):
    n, s, _ = x.shape
    n_t = s // t
    n_tiles = n * n_t
    groups = t // SUBLANES
    n_lt = D_A // LANES
    mix_tile = lambda j: jnp.minimum(j, n_tiles - 1)
    ffn_tile = lambda j: jnp.maximum(j - 1, 0)
    x_in = pl.BlockSpec((None, t, D_MODEL), lambda j: (mix_tile(j) // n_t, mix_tile(j) % n_t, 0))
    x_out = pl.BlockSpec((None, t, D_MODEL), lambda j: (ffn_tile(j) // n_t, ffn_tile(j) % n_t, 0))
    state = lambda g, ch, tile: pl.BlockSpec((None, g, SUBLANES, ch), lambda j: (tile(j) // n_t, 0, 0, 0))
    return pl.pallas_call(
        functools.partial(_layer_kernel, t=t, n_t=n_t, time_strided=time_strided, permute_in=permute_in,
                          permute_out=permute_out, final=final),
        grid=(n_tiles + 1,),
        in_specs=[
            x_in, state(HIST_A, D_A, mix_tile), state(HIST_3, D_B, mix_tile), state(HIST_3, D_FF, ffn_tile),
            _resident((1, D_MODEL), layer), _resident((D_MODEL, D_IN), layer),
            _resident((1, 2 * D_MODEL), layer),
            _resident((n_lt, CONV_A, LANES), layer),
            _resident((1, D_A), layer), _resident((1, D_A), layer), _resident((1, D_A), layer),
            _resident((D_A, D_MODEL), layer), _resident((CONV_B, D_B), layer),
            _resident((D_B, D_MODEL), layer), _resident((D_MODEL, D_MODEL), layer),
            _resident((1, D_MODEL), layer), _resident((D_MODEL, 2 * D_FF), layer),
            _resident((CONV_F, D_FF), layer), _resident((D_FF, D_MODEL), layer),
            pl.BlockSpec((1, D_MODEL), lambda j: (0, 0)),
        ],
        out_specs=[x_out, state(HIST_A, D_A, mix_tile), state(HIST_3, D_B, mix_tile),
                   state(HIST_3, D_FF, ffn_tile)],
        out_shape=[
            jax.ShapeDtypeStruct(x.shape, _F32),
            jax.ShapeDtypeStruct((n, HIST_A, SUBLANES, D_A), _F32),
            jax.ShapeDtypeStruct((n, HIST_3, SUBLANES, D_B), _F32),
            jax.ShapeDtypeStruct((n, HIST_3, SUBLANES, D_FF), _F32),
        ],
        scratch_shapes=[
            pltpu.VMEM((n_lt, HIST_A + groups, SUBLANES, LANES), _F32),
            pltpu.VMEM((n_lt, groups, SUBLANES, LANES), _F32),
            pltpu.VMEM((HIST_3, SUBLANES, D_B), _F32),
            pltpu.VMEM((HIST_3, SUBLANES, D_FF), _F32),
            pltpu.VMEM((2, t, D_MODEL), _F32),
            pltpu.VMEM((2, t, D_MODEL), _BF16),
        ],
        compiler_params=pltpu.CompilerParams(
            dimension_semantics=("arbitrary",), vmem_limit_bytes=VMEM_LIMIT_BYTES),
        name="layer",
    )(x, ha, hb, hf, p["norm1_g"], p["w_in"], p["b_gate"], p["dw_a"], p["b_dw_a"],
      p["ln_a_g"], p["ln_a_b"], p["w_a_out"], p["dw_b"], p["w_b_out"], p["w_o"], p["norm2_g"], p["w_up"],
      p["dw_f"], p["w_down"], p["final_g"])


class _Layout:
    def __init__(self, n, s, time_strided):
        self.n, self.s, self.time_strided = n, s, time_strided
        if time_strided:
            self.t = min(s, BLOCK_ROWS)
            self.blocks = n
            assert s % self.t == 0
        else:
            self.t = min(SUBLANES * s, BLOCK_ROWS)
            self.blocks = n // SUBLANES
            assert n % SUBLANES == 0 and (SUBLANES * s) % self.t == 0
        self.p = self.t // SUBLANES
        assert self.p >= HIST_A and self.p % CONV_GROUPS == 0

    def to_rows(self, x):
        c = x.shape[-1]
        if self.time_strided:
            return x
        y = x.reshape(self.blocks, SUBLANES, self.s, c).transpose(0, 2, 1, 3)
        return y.reshape(self.blocks, SUBLANES * self.s, c)

    def from_rows(self, y):
        c = y.shape[-1]
        if self.time_strided:
            return y
        x = y.reshape(self.blocks, self.s, SUBLANES, c).transpose(0, 2, 1, 3)
        return x.reshape(self.n, self.s, c)

    def state_in(self, cache, groups):
        n, w, c = cache.shape
        if self.time_strided:
            st = jnp.zeros((n, groups, SUBLANES, c), cache.dtype)
            return st.at[:, groups - w:, SUBLANES - 1, :].set(cache)
        st = cache.reshape(self.blocks, SUBLANES, w, c).transpose(0, 2, 1, 3)
        return jnp.pad(st, ((0, 0), (groups - w, 0), (0, 0), (0, 0)))

    def state_out(self, tail, w):
        groups, c = tail.shape[1], tail.shape[-1]
        if self.time_strided:
            return tail[:, groups - w:, SUBLANES - 1, :]
        return tail[:, groups - w:].transpose(0, 2, 1, 3).reshape(self.n, w, c)


def kernel(x_prompt, x_sample, cache_conv_a, cache_conv_b, cache_ffn_conv, norm1_g, w_in, b_gate,
           dw_a, b_dw_a, ln_a_g, ln_a_b, w_a_out, dw_b, w_b_out, w_o, norm2_g, w_up, dw_f,
           w_down, final_g):
    depth = w_in.shape[0]
    n_p = x_prompt.shape[0]
    row = lambda a: a.reshape(a.shape[0], 1, a.shape[1])
    p = dict(
        norm1_g=row(norm1_g), w_in=w_in.astype(_BF16), b_gate=row(b_gate),
        dw_a=dw_a.reshape(depth, CONV_A, D_A // LANES, LANES).transpose(0, 2, 1, 3),
        b_dw_a=row(b_dw_a), ln_a_g=row(ln_a_g), ln_a_b=row(ln_a_b), w_a_out=w_a_out.astype(_BF16),
        dw_b=dw_b, w_b_out=w_b_out.astype(_BF16), w_o=w_o.astype(_BF16), norm2_g=row(norm2_g),
        w_up=w_up.astype(_BF16), dw_f=dw_f,
        w_down=w_down.astype(_BF16), final_g=final_g.reshape(1, D_MODEL),
    )
    groups = [
        (_Layout(n_p, x_prompt.shape[1], True), x_prompt,
         lambda l: (jnp.zeros((n_p, CONV_A - 1, D_A), _F32), jnp.zeros((n_p, CONV_B - 1, D_B), _F32),
                    jnp.zeros((n_p, CONV_F - 1, D_FF), _F32))),
        (_Layout(x_sample.shape[0], x_sample.shape[1], False), x_sample,
         lambda l: (cache_conv_a[l], cache_conv_b[l], cache_ffn_conv[l])),
    ]
    ys, new_a, new_b, new_f = [], [], [], []
    for lay, x, caches in groups:
        xr = lay.to_rows(x)
        sa, sb, sf = [], [], []
        for l in range(depth):
            ca, cb, cf = caches(l)
            xr, ta, tb, tf = _layer(xr, lay.state_in(ca, HIST_A), lay.state_in(cb, HIST_3),
                                    lay.state_in(cf, HIST_3), l, p, lay.t, lay.time_strided,
                                    lay.time_strided and l == 0, lay.time_strided and l == depth - 1,
                                    l == depth - 1)
            sa.append(lay.state_out(ta, CONV_A - 1))
            sb.append(lay.state_out(tb, CONV_B - 1))
            sf.append(lay.state_out(tf, CONV_F - 1))
        ys.append(lay.from_rows(xr))
        new_a.append(jnp.stack(sa)); new_b.append(jnp.stack(sb)); new_f.append(jnp.stack(sf))
    return (ys[0], ys[1], new_a[0], new_b[0], new_f[0], new_a[1], new_b[1], new_f[1])
```
